```python
import math
import jax
import jax.numpy as jnp
from jax import lax
import numpy as np

D_MODEL = 1024
BATCH = 4
SEQ = 4096
DEPTH = 4

MIX_WIDTH = D_MODEL
HEAD_DIM = 64
DA_WIDTH = MIX_WIDTH // 2
DA_HEADS = DA_WIDTH // (2 * HEAD_DIM)
SB_WIDTH = MIX_WIDTH - DA_WIDTH
SB_HEADS = SB_WIDTH // HEAD_DIM
ROPE_DIM = HEAD_DIM // 4
ROPE_THETA = 500000.0
Q_BLOCK = 128
N_GROUPS = 4
EXPERTS_PER_GROUP = 8
N_EXPERTS = N_GROUPS * EXPERTS_PER_GROUP
TOP_K_IN_GROUP = 2
EXPERT_FF = D_MODEL // 4
RMS_EPS = 1e-6
PROJ_WIDTH = 3 * DA_WIDTH + 3 * SB_WIDTH
SPLITS = [DA_WIDTH, 2 * DA_WIDTH, 3 * DA_WIDTH, 3 * DA_WIDTH + SB_WIDTH, 3 * DA_WIDTH + 2 * SB_WIDTH]

kernel_name = 'hybrid_diffattn_stickbreak_hmoe'


def rms_norm(x, g):
    xf = x.astype(jnp.float32)
    y = xf * lax.rsqrt(jnp.mean(xf * xf, axis=-1, keepdims=True) + RMS_EPS)
    return (y * g.astype(jnp.float32)).astype(x.dtype)


def split_heads(t, n_heads, head_dim):
    b, s, _ = t.shape
    return t.reshape(b, s, n_heads, head_dim).transpose(0, 2, 1, 3)


def merge_heads(t):
    b, h, s, d = t.shape
    return t.transpose(0, 2, 1, 3).reshape(b, s, h * d)


def rope_tables(seq):
    inv_freq = 1.0 / (ROPE_THETA ** (jnp.arange(0, ROPE_DIM, 2, dtype=jnp.float32) / ROPE_DIM))
    ang = jnp.arange(seq, dtype=jnp.float32)[:, None] * inv_freq[None, :]
    return jnp.cos(ang), jnp.sin(ang)


def partial_rope(t, cos, sin):
    half = ROPE_DIM // 2
    x1 = t[..., :half]
    x2 = t[..., half:ROPE_DIM]
    c = cos.astype(t.dtype)
    s = sin.astype(t.dtype)
    return jnp.concatenate([x1 * c - x2 * s, x2 * c + x1 * s, t[..., ROPE_DIM:]], axis=-1)


def to_query_blocks(q):
    b, h, s, d = q.shape
    return q.reshape(b, h, s // Q_BLOCK, Q_BLOCK, d).transpose(2, 0, 1, 3, 4)


def from_query_blocks(o):
    nb, b, h, qb, d = o.shape
    return o.transpose(1, 2, 0, 3, 4).reshape(b, h, nb * qb, d)


def diff_attention(q, k, v, lam):
    b, h2, s, d = q.shape
    h = h2 // 2
    scale = d ** -0.5
    kpos = jnp.arange(s)

    def block(args):
        qi, i = args
        sc = jnp.einsum('bhqd,bhkd->bhqk', qi, k).astype(jnp.float32) * scale
        qpos = i * Q_BLOCK + jnp.arange(Q_BLOCK)
        sc = jnp.where(kpos[None, :] <= qpos[:, None], sc, -jnp.inf)
        p = jax.nn.softmax(sc, axis=-1).reshape(b, h, 2, Q_BLOCK, s)
        a = p[:, :, 0] - lam * p[:, :, 1]
        return jnp.einsum('bhqk,bhkd->bhqd', a.astype(v.dtype), v)

    o = lax.map(block, (to_query_blocks(q), jnp.arange(s // Q_BLOCK)))
    return from_query_blocks(o)


def stick_breaking_attention(q, k, v):
    b, h, s, d = q.shape
    scale = d ** -0.5
    kpos = jnp.arange(s)

    def block(args):
        qi, i = args
        z = jnp.einsum('bhqd,bhkd->bhqk', qi, k).astype(jnp.float32) * scale
        qpos = i * Q_BLOCK + jnp.arange(Q_BLOCK)
        strict = kpos[None, :] < qpos[:, None]
        log_fail = jnp.where(strict, jax.nn.log_sigmoid(-z), 0.0)
        after = lax.cumsum(log_fail, axis=3, reverse=True) - log_fail
        log_a = jax.nn.log_sigmoid(z) + after
        a = jnp.where(strict, jnp.exp(log_a), 0.0)
        return jnp.einsum('bhqk,bhkd->bhqd', a.astype(v.dtype), v)

    o = lax.map(block, (to_query_blocks(q), jnp.arange(s // Q_BLOCK)))
    return from_query_blocks(o)


def hierarchical_moe(h, w_rg, b_rg, w_re, b_re, w_gate, w_up, w_down):
    def per_sequence(hs):
        n = hs.shape[0]
        g_prob = jax.nn.softmax((hs @ w_rg + b_rg).astype(jnp.float32), axis=-1)
        g_p, g_idx = lax.top_k(g_prob, 1)
        e_logits = (hs @ w_re + b_re).astype(jnp.float32).reshape(n, N_GROUPS, EXPERTS_PER_GROUP)
        idx = jnp.broadcast_to(g_idx[:, :, None], (n, 1, EXPERTS_PER_GROUP))
        e_in = jnp.take_along_axis(e_logits, idx, axis=1)[:, 0]
        e_top, e_idx = lax.top_k(e_in, TOP_K_IN_GROUP)
        w = g_p * jax.nn.softmax(e_top, axis=-1)
        expert_id = g_idx * EXPERTS_PER_GROUP + e_idx
        combine = jnp.sum(jax.nn.one_hot(expert_id, N_EXPERTS, dtype=jnp.float32) * w[..., None], axis=1)
        hg = jnp.einsum('sd,edf->sef', hs, w_gate)
        hu = jnp.einsum('sd,edf->sef', hs, w_up)
        act = jax.nn.silu(hg) * hu * combine.astype(hs.dtype)[..., None]
        return jnp.einsum('sef,efd->sd', act, w_down)

    return lax.map(per_sequence, h)


def setup_inputs(seed: int = 0) -> dict:
    key = jax.random.key(seed)
    ks = jax.random.split(key, 20)
    L, D = DEPTH, D_MODEL
    f32 = jnp.float32

    def normal(k, shape, scale):
        return jax.random.normal(k, shape, f32) * scale

    def gain(k, shape):
        return 1.0 + 0.02 * jax.random.normal(k, shape, f32)

    return {
        'x': normal(ks[0], (BATCH, SEQ, D), 1.0),
        'norm1_g': gain(ks[1], (L, D)),
        'w_in': normal(ks[2], (L, D, PROJ_WIDTH), D ** -0.5),
        'q_norm_g': gain(ks[3], (L, HEAD_DIM)),
        'k_norm_g': gain(ks[4], (L, HEAD_DIM)),
        'lambda_q1': normal(ks[5], (L, HEAD_DIM), 0.1),
        'lambda_k1': normal(ks[6], (L, HEAD_DIM), 0.1),
        'lambda_q2': normal(ks[7], (L, HEAD_DIM), 0.1),
        'lambda_k2': normal(ks[8], (L, HEAD_DIM), 0.1),
        'da_out_g': gain(ks[9], (L, 2 * HEAD_DIM)),
        'sb_out_g': gain(ks[10], (L, HEAD_DIM)),
        'w_out': normal(ks[11], (L, MIX_WIDTH, D), MIX_WIDTH ** -0.5),
        'norm2_g': gain(ks[12], (L, D)),
        'w_router_group': normal(ks[13], (L, D, N_GROUPS), D ** -0.5),
        'b_router_group': normal(ks[14], (L, N_GROUPS), 0.01),
        'w_router_expert': normal(ks[15], (L, D, N_EXPERTS), D ** -0.5),
        'b_router_expert': normal(ks[16], (L, N_EXPERTS), 0.01),
        'w_gate': normal(ks[17], (L, N_EXPERTS, D, EXPERT_FF), D ** -0.5),
        'w_up': normal(ks[18], (L, N_EXPERTS, D, EXPERT_FF), D ** -0.5),
        'w_down': normal(ks[19], (L, N_EXPERTS, EXPERT_FF, D), EXPERT_FF ** -0.5),
    }


def reference(x, norm1_g, w_in, q_norm_g, k_norm_g, lambda_q1, lambda_k1, lambda_q2, lambda_k2,
              da_out_g, sb_out_g, w_out, norm2_g, w_router_group, b_router_group,
              w_router_expert, b_router_expert, w_gate, w_up, w_down):
    _, s, _ = x.shape
    cos, sin = rope_tables(s)
    for l in range(DEPTH):
        lambda_init = 0.8 - 0.6 * math.exp(-0.3 * l)
        h = rms_norm(x, norm1_g[l])
        proj = h @ w_in[l]
        da_q, da_k, da_v, sb_q, sb_k, sb_v = jnp.split(proj, SPLITS, axis=-1)

        q = partial_rope(rms_norm(split_heads(da_q, 2 * DA_HEADS, HEAD_DIM), q_norm_g[l]), cos, sin)
        k = partial_rope(rms_norm(split_heads(da_k, 2 * DA_HEADS, HEAD_DIM), k_norm_g[l]), cos, sin)
        v = split_heads(da_v, DA_HEADS, 2 * HEAD_DIM)
        lam = (jnp.exp(jnp.sum(lambda_q1[l].astype(jnp.float32) * lambda_k1[l].astype(jnp.float32)))
               - jnp.exp(jnp.sum(lambda_q2[l].astype(jnp.float32) * lambda_k2[l].astype(jnp.float32)))
               + lambda_init)
        o_da = diff_attention(q, k, v, lam)
        o_da = merge_heads(rms_norm(o_da, da_out_g[l]) * (1.0 - lambda_init))

        o_sb = stick_breaking_attention(split_heads(sb_q, SB_HEADS, HEAD_DIM),
                                        split_heads(sb_k, SB_HEADS, HEAD_DIM),
                                        split_heads(sb_v, SB_HEADS, HEAD_DIM))
        o_sb = merge_heads(rms_norm(o_sb, sb_out_g[l]))

        x = x + jnp.concatenate([o_da, o_sb], axis=-1) @ w_out[l]

        h2 = rms_norm(x, norm2_g[l])
        x = x + hierarchical_moe(h2, w_router_group[l], b_router_group[l], w_router_expert[l],
                                 b_router_expert[l], w_gate[l], w_up[l], w_down[l])
    return x
```

```python
import functools
import math

import jax
import jax.numpy as jnp
from jax import lax
from jax.experimental import pallas as pl
from jax.experimental.pallas import tpu as pltpu

F32 = jnp.float32
BF16 = jnp.bfloat16

D_MODEL = 1024
HEAD_DIM = 64
GROUP_WIDTH = 512
ROPE_HALF = 8
N_GROUPS = 4
EXPERTS_PER_GROUP = 8
N_EXPERTS = N_GROUPS * EXPERTS_PER_GROUP
EXPERT_FF = 256
RMS_EPS = 1e-6
ROPE_THETA = 500000.0
LANES = 128

ROW_TILE = 512
ATT_BLOCK = 256
EXPERT_TILE = 256
MOVE_ROWS = 256
VMEM_LIMIT = 48 * 1024 * 1024


def _split_bf16(x):
    hi = x.astype(BF16)
    lo = (x - hi.astype(F32)).astype(BF16)
    return hi, lo


def _inproj_kernel(x_ref, g1_ref, w_ref, qg_ref, kg_ref, cos_ref, sa_ref, sb_ref, seg_ref,
                   daq_ref, dak_ref, dav_ref, sbq_ref, sbk_ref, sbv_ref):
    x = x_ref[...]
    ms = jnp.mean(x * x, axis=-1, keepdims=True)
    h = (x * lax.rsqrt(ms + RMS_EPS) * g1_ref[...]).astype(BF16)

    def proj(c):
        return jnp.dot(h, w_ref[:, c * GROUP_WIDTH:(c + 1) * GROUP_WIDTH],
                       preferred_element_type=F32)

    cos_t = cos_ref[...]
    sa_t = sa_ref[...]
    sb_t = sb_ref[...]
    seg = seg_ref[...]

    def head_norm_rope(p, g):
        hi, lo = _split_bf16(p * p)
        ssum = (jnp.dot(hi, seg, preferred_element_type=F32)
                + jnp.dot(lo, seg, preferred_element_type=F32))
        n = p * lax.rsqrt(ssum * (1.0 / HEAD_DIM) + RMS_EPS) * g
        outs = []
        for c in range(GROUP_WIDTH // LANES):
            t = n[:, c * LANES:(c + 1) * LANES]
            up = pltpu.roll(t, LANES - ROPE_HALF, 1)
            dn = pltpu.roll(t, ROPE_HALF, 1)
            outs.append(t * cos_t + up * sa_t + dn * sb_t)
        return jnp.concatenate(outs, axis=1)

    scale = HEAD_DIM ** -0.5
    daq_ref[...] = (head_norm_rope(proj(0), qg_ref[...]) * scale).astype(BF16)
    dak_ref[...] = head_norm_rope(proj(1), kg_ref[...]).astype(BF16)
    dav_ref[...] = proj(2).astype(BF16)
    sbq_ref[...] = (proj(3) * scale).astype(BF16)
    sbk_ref[...] = proj(4).astype(BF16)
    sbv_ref[...] = proj(5).astype(BF16)


def _inproj(x2, g1, w_bf, qg, kg, cos_t, sa_t, sb_t, seg, seq):
    t = x2.shape[0]
    n_tiles = t // ROW_TILE
    seq_tiles = seq // ROW_TILE
    row = lambda i: (i, 0)
    const = lambda i: (0, 0)
    pos = lambda i: (i % seq_tiles, 0)
    out = jax.ShapeDtypeStruct((t, GROUP_WIDTH), BF16)
    return pl.pallas_call(
        _inproj_kernel,
        out_shape=[out] * 6,
        grid=(n_tiles,),
        in_specs=[
            pl.BlockSpec((ROW_TILE, D_MODEL), row),
            pl.BlockSpec((1, D_MODEL), const),
            pl.BlockSpec((D_MODEL, 6 * GROUP_WIDTH), const),
            pl.BlockSpec((1, GROUP_WIDTH), const),
            pl.BlockSpec((1, GROUP_WIDTH), const),
            pl.BlockSpec((ROW_TILE, LANES), pos),
            pl.BlockSpec((ROW_TILE, LANES), pos),
            pl.BlockSpec((ROW_TILE, LANES), pos),
            pl.BlockSpec((GROUP_WIDTH, GROUP_WIDTH), const),
        ],
        out_specs=[pl.BlockSpec((ROW_TILE, GROUP_WIDTH), row)] * 6,
        compiler_params=pltpu.CompilerParams(
            dimension_semantics=("parallel",), vmem_limit_bytes=VMEM_LIMIT),
        name="inproj",
    )(x2, g1, w_bf, qg, kg, cos_t, sa_t, sb_t, seg)


def _da_kernel(lam_ref, q_ref, k_ref, v_ref, g_ref, o_ref, *, out_scale):
    i = pl.program_id(2)
    blk = ATT_BLOCK
    q = q_ref[...]
    row = lax.broadcasted_iota(jnp.int32, (blk, blk), 0)
    col = lax.broadcasted_iota(jnp.int32, (blk, blk), 1)
    causal = col <= row

    def step(j, carry, masked):
        new = []
        vj = v_ref[pl.ds(j * blk, blk), :]
        for s in range(2):
            m, l, acc = carry[s]
            qs = q[:, s * HEAD_DIM:(s + 1) * HEAD_DIM]
            kj = k_ref[pl.ds(j * blk, blk), s * HEAD_DIM:(s + 1) * HEAD_DIM]
            sc = lax.dot_general(qs, kj, (((1,), (1,)), ((), ())), preferred_element_type=F32)
            if masked:
                sc = jnp.where(causal, sc, -jnp.inf)
            m_new = jnp.maximum(m, jnp.max(sc, axis=-1, keepdims=True))
            p = jnp.exp(sc - m_new)
            alpha = jnp.exp(m - m_new)
            l = alpha * l + jnp.sum(p, axis=-1, keepdims=True)
            acc = alpha * acc + jnp.dot(p.astype(BF16), vj, preferred_element_type=F32)
            new.append((m_new, l, acc))
        return tuple(new)

    init = tuple((jnp.full((blk, 1), -jnp.inf, F32), jnp.zeros((blk, 1), F32),
                  jnp.zeros((blk, 2 * HEAD_DIM), F32)) for _ in range(2))
    carry = lax.fori_loop(0, i, lambda j, c: step(j, c, False), init)
    (_, l0, a0), (_, l1, a1) = step(i, carry, True)
    o = a0 / l0 - lam_ref[0] * (a1 / l1)
    ms = jnp.mean(o * o, axis=-1, keepdims=True)
    o_ref[...] = (o * lax.rsqrt(ms + RMS_EPS) * g_ref[...] * out_scale).astype(BF16)


def _diff_attention(lam, q, k, v, g, batch, seq, out_scale):
    nq = seq // ATT_BLOCK
    heads = GROUP_WIDTH // (2 * HEAD_DIM)
    return pl.pallas_call(
        functools.partial(_da_kernel, out_scale=out_scale),
        out_shape=jax.ShapeDtypeStruct(q.shape, BF16),
        grid=(batch, heads, nq),
        in_specs=[
            pl.BlockSpec(memory_space=pltpu.SMEM),
            pl.BlockSpec((ATT_BLOCK, 2 * HEAD_DIM), lambda b, h, i: (b * nq + i, h)),
            pl.BlockSpec((seq, 2 * HEAD_DIM), lambda b, h, i: (b, h)),
            pl.BlockSpec((seq, 2 * HEAD_DIM), lambda b, h, i: (b, h)),
            pl.BlockSpec((1, 2 * HEAD_DIM), lambda b, h, i: (0, 0)),
        ],
        out_specs=pl.BlockSpec((ATT_BLOCK, 2 * HEAD_DIM), lambda b, h, i: (b * nq + i, h)),
        compiler_params=pltpu.CompilerParams(
            dimension_semantics=("parallel", "parallel", "parallel"),
            vmem_limit_bytes=VMEM_LIMIT),
        name="diff_attention",
    )(lam, q, k, v, g)


def _sb_kernel(q_ref, k_ref, v_ref, tri_ref, g_ref, o_ref):
    i = pl.program_id(2)
    blk = ATT_BLOCK
    q = q_ref[...]
    tri = tri_ref[...]
    row = lax.broadcasted_iota(jnp.int32, (blk, blk), 0)
    col = lax.broadcasted_iota(jnp.int32, (blk, blk), 1)
    strict = col < row

    def step(j, carry, masked):
        new = []
        for s in range(2):
            tail, acc = carry[s]
            qs = q[:, s * HEAD_DIM:(s + 1) * HEAD_DIM]
            kj = k_ref[pl.ds(j * blk, blk), s * HEAD_DIM:(s + 1) * HEAD_DIM]
            vj = v_ref[pl.ds(j * blk, blk), s * HEAD_DIM:(s + 1) * HEAD_DIM]
            z = lax.dot_general(qs, kj, (((1,), (1,)), ((), ())), preferred_element_type=F32)
            sp = jnp.maximum(z, 0.0) + jnp.log(1.0 + jnp.exp(-jnp.abs(z)))
            if masked:
                sp = jnp.where(strict, sp, 0.0)
            hi, lo = _split_bf16(sp)
            suffix = (jnp.dot(hi, tri, preferred_element_type=F32)
                      + jnp.dot(lo, tri, preferred_element_type=F32))
            a = jnp.exp(z - suffix - tail)
            if masked:
                a = jnp.where(strict, a, 0.0)
            acc = acc + jnp.dot(a.astype(BF16), vj, preferred_element_type=F32)
            new.append((tail + suffix[:, 0:1], acc))
        return tuple(new)

    init = tuple((jnp.zeros((blk, 1), F32), jnp.zeros((blk, HEAD_DIM), F32)) for _ in range(2))
    carry = step(i, init, True)
    carry = lax.fori_loop(0, i, lambda t, c: step(i - 1 - t, c, False), carry)
    g = g_ref[...]
    outs = []
    for s in range(2):
        o = carry[s][1]
        ms = jnp.mean(o * o, axis=-1, keepdims=True)
        outs.append(o * lax.rsqrt(ms + RMS_EPS) * g)
    o_ref[...] = jnp.concatenate(outs, axis=1).astype(BF16)


def _stick_breaking_attention(q, k, v, tri, g, batch, seq):
    nq = seq // ATT_BLOCK
    pairs = GROUP_WIDTH // (2 * HEAD_DIM)
    return pl.pallas_call(
        _sb_kernel,
        out_shape=jax.ShapeDtypeStruct(q.shape, BF16),
        grid=(batch, pairs, nq),
        in_specs=[
            pl.BlockSpec((ATT_BLOCK, 2 * HEAD_DIM), lambda b, h, i: (b * nq + i, h)),
            pl.BlockSpec((seq, 2 * HEAD_DIM), lambda b, h, i: (b, h)),
            pl.BlockSpec((seq, 2 * HEAD_DIM), lambda b, h, i: (b, h)),
            pl.BlockSpec((ATT_BLOCK, ATT_BLOCK), lambda b, h, i: (0, 0)),
            pl.BlockSpec((1, HEAD_DIM), lambda b, h, i: (0, 0)),
        ],
        out_specs=pl.BlockSpec((ATT_BLOCK, 2 * HEAD_DIM), lambda b, h, i: (b * nq + i, h)),
        compiler_params=pltpu.CompilerParams(
            dimension_semantics=("parallel", "parallel", "parallel"),
            vmem_limit_bytes=VMEM_LIMIT),
        name="stick_breaking_attention",
    )(q, k, v, tri, g)


def _outproj_kernel(x_ref, oda_ref, osb_ref, w_ref, g2_ref, wrh_ref, wrl_ref, br_ref,
                    x1_ref, h2_ref, ids_ref, wts_ref):
    mix = (jnp.dot(oda_ref[...], w_ref[:GROUP_WIDTH, :], preferred_element_type=F32)
           + jnp.dot(osb_ref[...], w_ref[GROUP_WIDTH:, :], preferred_element_type=F32))
    x1 = x_ref[...] + mix
    x1_ref[...] = x1
    ms = jnp.mean(x1 * x1, axis=-1, keepdims=True)
    h2 = x1 * lax.rsqrt(ms + RMS_EPS) * g2_ref[...]
    h2_ref[...] = h2

    hi, lo = _split_bf16(h2)
    wrh = wrh_ref[...]
    logits = (jnp.dot(hi, wrh, preferred_element_type=F32)
              + jnp.dot(hi, wrl_ref[...], preferred_element_type=F32)
              + jnp.dot(lo, wrh, preferred_element_type=F32)) + br_ref[...]

    rows = logits.shape[0]
    lane = lax.broadcasted_iota(jnp.int32, (rows, LANES), 1).astype(F32)
    neg = -jnp.inf
    gl = jnp.where(lane < N_GROUPS, logits, neg)
    gmax = jnp.max(gl, axis=-1, keepdims=True)
    g_p = 1.0 / jnp.sum(jnp.exp(gl - gmax), axis=-1, keepdims=True)
    g_idx = jnp.min(jnp.where(gl == gmax, lane, float(LANES)), axis=-1, keepdims=True)
    first = N_GROUPS + g_idx * EXPERTS_PER_GROUP
    el = jnp.where((lane >= first) & (lane < first + EXPERTS_PER_GROUP), logits, neg)
    e1 = jnp.max(el, axis=-1, keepdims=True)
    i1 = jnp.min(jnp.where(el == e1, lane, float(LANES)), axis=-1, keepdims=True)
    el2 = jnp.where(lane == i1, neg, el)
    e2 = jnp.max(el2, axis=-1, keepdims=True)
    i2 = jnp.min(jnp.where(el2 == e2, lane, float(LANES)), axis=-1, keepdims=True)
    d = jnp.exp(e2 - e1)
    w1 = g_p / (1.0 + d)
    w2 = g_p * d / (1.0 + d)
    ids = jnp.where(lane == 0.0, i1 - N_GROUPS, jnp.where(lane == 1.0, i2 - N_GROUPS, 0.0))
    ids_ref[...] = ids.astype(jnp.int32)
    wts_ref[...] = jnp.where(lane == 0.0, w1, jnp.where(lane == 1.0, w2, 0.0))


def _outproj(x2, oda, osb, w_bf, g2, wr_hi, wr_lo, br):
    t = x2.shape[0]
    row = lambda i: (i, 0)
    const = lambda i: (0, 0)
    return pl.pallas_call(
        _outproj_kernel,
        out_shape=[
            jax.ShapeDtypeStruct((t, D_MODEL), F32),
            jax.ShapeDtypeStruct((t, D_MODEL), F32),
            jax.ShapeDtypeStruct((t, LANES), jnp.int32),
            jax.ShapeDtypeStruct((t, LANES), F32),
        ],
        grid=(t // ROW_TILE,),
        in_specs=[
            pl.BlockSpec((ROW_TILE, D_MODEL), row),
            pl.BlockSpec((ROW_TILE, GROUP_WIDTH), row),
            pl.BlockSpec((ROW_TILE, GROUP_WIDTH), row),
            pl.BlockSpec((D_MODEL, D_MODEL), const),
            pl.BlockSpec((1, D_MODEL), const),
            pl.BlockSpec((D_MODEL, LANES), const),
            pl.BlockSpec((D_MODEL, LANES), const),
            pl.BlockSpec((1, LANES), const),
        ],
        out_specs=[
            pl.BlockSpec((ROW_TILE, D_MODEL), row),
            pl.BlockSpec((ROW_TILE, D_MODEL), row),
            pl.BlockSpec((ROW_TILE, LANES), row),
            pl.BlockSpec((ROW_TILE, LANES), row),
        ],
        compiler_params=pltpu.CompilerParams(
            dimension_semantics=("parallel",), vmem_limit_bytes=VMEM_LIMIT),
        name="outproj_router",
    )(x2, oda, osb, w_bf, g2, wr_hi, wr_lo, br)


def _row_copy(src, src_row, dst, dst_row, sem):
    return pltpu.make_async_copy(src.at[pl.ds(src_row, 1)], dst.at[pl.ds(dst_row, 1)], sem)


def _dispatch_kernel(pos_ref, h_ref, zero_ref, xs_ref, sem):
    del zero_ref
    base = pl.program_id(0) * MOVE_ROWS

    def issue(r, c):
        for s in range(2):
            _row_copy(h_ref, base + r, xs_ref, pos_ref[0, 0, 2 * r + s], sem).start()
        return c

    def drain(r, c):
        for s in range(2):
            _row_copy(h_ref, 0, xs_ref, 0, sem).wait()
        return c

    lax.fori_loop(0, MOVE_ROWS, issue, 0)
    lax.fori_loop(0, MOVE_ROWS, drain, 0)


def _dispatch(pos3, h2, zeros):
    t = h2.shape[0]
    return pl.pallas_call(
        _dispatch_kernel,
        out_shape=jax.ShapeDtypeStruct(zeros.shape, F32),
        grid=(t // MOVE_ROWS,),
        in_specs=[
            pl.BlockSpec((1, 1, 2 * MOVE_ROWS), lambda i: (i, 0, 0), memory_space=pltpu.SMEM),
            pl.BlockSpec(memory_space=pl.ANY),
            pl.BlockSpec(memory_space=pl.ANY),
        ],
        out_specs=pl.BlockSpec(memory_space=pl.ANY),
        scratch_shapes=[pltpu.SemaphoreType.DMA(())],
        input_output_aliases={2: 0},
        compiler_params=pltpu.CompilerParams(
            dimension_semantics=("arbitrary",), has_side_effects=True),
        name="dispatch",
    )(pos3, h2, zeros)


def _experts_kernel(te_ref, nv_ref, xs_ref, wg_ref, wu_ref, wd_ref, ys_ref):
    i = pl.program_id(0)

    @pl.when(i < nv_ref[0])
    def _():
        x = xs_ref[...].astype(BF16)
        hg = jnp.dot(x, wg_ref[0].astype(BF16), preferred_element_type=F32)
        hu = jnp.dot(x, wu_ref[0].astype(BF16), preferred_element_type=F32)
        act = (hg * jax.nn.sigmoid(hg) * hu).astype(BF16)
        ys_ref[...] = jnp.dot(act, wd_ref[0].astype(BF16), preferred_element_type=F32)

    @pl.when(i >= nv_ref[0])
    def _():
        ys_ref[...] = jnp.zeros_like(ys_ref)


def _experts(tile_expert, n_valid, xs, w_gate, w_up, w_down):
    p = xs.shape[0]
    grid_spec = pltpu.PrefetchScalarGridSpec(
        num_scalar_prefetch=2,
        grid=(p // EXPERT_TILE,),
        in_specs=[
            pl.BlockSpec((EXPERT_TILE, D_MODEL), lambda i, te, nv: (i, 0)),
            pl.BlockSpec((1, D_MODEL, EXPERT_FF), lambda i, te, nv: (te[i], 0, 0)),
            pl.BlockSpec((1, D_MODEL, EXPERT_FF), lambda i, te, nv: (te[i], 0, 0)),
            pl.BlockSpec((1, EXPERT_FF, D_MODEL), lambda i, te, nv: (te[i], 0, 0)),
        ],
        out_specs=pl.BlockSpec((EXPERT_TILE, D_MODEL), lambda i, te, nv: (i, 0)),
    )
    return pl.pallas_call(
        _experts_kernel,
        out_shape=jax.ShapeDtypeStruct((p, D_MODEL), F32),
        grid_spec=grid_spec,
        compiler_params=pltpu.CompilerParams(
            dimension_semantics=("arbitrary",), vmem_limit_bytes=VMEM_LIMIT),
        name="experts",
    )(tile_expert, n_valid, xs, w_gate, w_up, w_down)


def _combine_kernel(pos_ref, x1_ref, wts_ref, ys_ref, out_ref, buf, sem):
    def issue(r, c):
        for s in range(2):
            _row_copy(ys_ref, pos_ref[0, 0, 2 * r + s], buf.at[s], r, sem).start()
        return c

    def drain(r, c):
        for s in range(2):
            _row_copy(ys_ref, 0, buf.at[s], 0, sem).wait()
        return c

    lax.fori_loop(0, MOVE_ROWS, issue, 0)
    lax.fori_loop(0, MOVE_ROWS, drain, 0)
    w = wts_ref[...]
    out_ref[...] = x1_ref[...] + w[:, 0:1] * buf[0] + w[:, 1:2] * buf[1]


def _combine(pos3, x1, wts, ys):
    t = x1.shape[0]
    return pl.pallas_call(
        _combine_kernel,
        out_shape=jax.ShapeDtypeStruct((t, D_MODEL), F32),
        grid=(t // MOVE_ROWS,),
        in_specs=[
            pl.BlockSpec((1, 1, 2 * MOVE_ROWS), lambda i: (i, 0, 0), memory_space=pltpu.SMEM),
            pl.BlockSpec((MOVE_ROWS, D_MODEL), lambda i: (i, 0)),
            pl.BlockSpec((MOVE_ROWS, LANES), lambda i: (i, 0)),
            pl.BlockSpec(memory_space=pl.ANY),
        ],
        out_specs=pl.BlockSpec((MOVE_ROWS, D_MODEL), lambda i: (i, 0)),
        scratch_shapes=[pltpu.VMEM((2, MOVE_ROWS, D_MODEL), F32), pltpu.SemaphoreType.DMA(())],
        compiler_params=pltpu.CompilerParams(
            dimension_semantics=("arbitrary",), vmem_limit_bytes=VMEM_LIMIT),
        name="combine",
    )(pos3, x1, wts, ys)


def _routing_positions(ids, n_tiles):
    e = ids.reshape(-1)
    onehot = (e[:, None] == jnp.arange(N_EXPERTS, dtype=jnp.int32)[None, :]).astype(jnp.int32)
    csum = jnp.cumsum(onehot, axis=0)
    rank = jnp.sum(csum * onehot, axis=1) - 1
    counts = csum[-1]
    padded = ((counts + EXPERT_TILE - 1) // EXPERT_TILE) * EXPERT_TILE
    ends = jnp.cumsum(padded)
    starts = ends - padded
    pos = jnp.sum(starts[None, :] * onehot, axis=1) + rank
    tile_start = jnp.arange(n_tiles, dtype=jnp.int32) * EXPERT_TILE
    tile_expert = jnp.sum((tile_start[:, None] >= ends[None, :]).astype(jnp.int32), axis=1)
    tile_expert = jnp.minimum(tile_expert, N_EXPERTS - 1).astype(jnp.int32)
    n_valid = (ends[-1] // EXPERT_TILE).astype(jnp.int32).reshape(1)
    return pos.astype(jnp.int32), tile_expert, n_valid


def _rope_tables(seq):
    inv_freq = 1.0 / (ROPE_THETA ** (jnp.arange(0, 2 * ROPE_HALF, 2, dtype=F32) / (2 * ROPE_HALF)))
    ang = jnp.arange(seq, dtype=F32)[:, None] * inv_freq[None, :]
    cos, sin = jnp.cos(ang), jnp.sin(ang)
    ones = jnp.ones((seq, HEAD_DIM - 2 * ROPE_HALF), F32)
    zeros8 = jnp.zeros((seq, ROPE_HALF), F32)
    zeros = jnp.zeros((seq, HEAD_DIM - 2 * ROPE_HALF), F32)
    cos_h = jnp.concatenate([cos, cos, ones], axis=1)
    sa_h = jnp.concatenate([-sin, zeros8, zeros], axis=1)
    sb_h = jnp.concatenate([zeros8, sin, zeros], axis=1)
    rep = LANES // HEAD_DIM
    return jnp.tile(cos_h, (1, rep)), jnp.tile(sa_h, (1, rep)), jnp.tile(sb_h, (1, rep))


def kernel(x, norm1_g, w_in, q_norm_g, k_norm_g, lambda_q1, lambda_k1, lambda_q2, lambda_k2,
           da_out_g, sb_out_g, w_out, norm2_g, w_router_group, b_router_group,
           w_router_expert, b_router_expert, w_gate, w_up, w_down):
    batch, seq, d = x.shape
    depth = w_in.shape[0]
    t = batch * seq
    assert d == D_MODEL and seq % ROW_TILE == 0 and seq % ATT_BLOCK == 0 and t % MOVE_ROWS == 0

    cos_t, sa_t, sb_t = _rope_tables(seq)
    head_of = jnp.arange(GROUP_WIDTH) // HEAD_DIM
    seg = (head_of[:, None] == head_of[None, :]).astype(BF16)
    kk = jnp.arange(ATT_BLOCK)
    tri = (kk[:, None] >= kk[None, :]).astype(BF16)
    n_rows = 2 * t + N_EXPERTS * EXPERT_TILE
    n_tiles = n_rows // EXPERT_TILE
    heads_per_group = GROUP_WIDTH // HEAD_DIM

    x2 = x.reshape(t, d)
    for l in range(depth):
        lambda_init = 0.8 - 0.6 * math.exp(-0.3 * l)
        lam = (jnp.exp(jnp.sum(lambda_q1[l] * lambda_k1[l]))
               - jnp.exp(jnp.sum(lambda_q2[l] * lambda_k2[l])) + lambda_init).reshape(1).astype(F32)

        daq, dak, dav, sbq, sbk, sbv = _inproj(
            x2, norm1_g[l][None, :], w_in[l].astype(BF16),
            jnp.tile(q_norm_g[l], heads_per_group)[None, :],
            jnp.tile(k_norm_g[l], heads_per_group)[None, :],
            cos_t, sa_t, sb_t, seg, seq)

        o_da = _diff_attention(lam, daq, dak, dav, da_out_g[l][None, :], batch, seq,
                               1.0 - lambda_init)
        o_sb = _stick_breaking_attention(sbq, sbk, sbv, tri, sb_out_g[l][None, :], batch, seq)

        wr = jnp.concatenate([w_router_group[l], w_router_expert[l]], axis=1)
        wr = jnp.pad(wr, ((0, 0), (0, LANES - wr.shape[1])))
        wr_hi, wr_lo = _split_bf16(wr)
        br = jnp.concatenate([b_router_group[l], b_router_expert[l]])
        br = jnp.pad(br, (0, LANES - br.shape[0]))[None, :]
        x1, h2, ids, wts = _outproj(x2, o_da, o_sb, w_out[l].astype(BF16), norm2_g[l][None, :],
                                    wr_hi, wr_lo, br)

        pos, tile_expert, n_valid = _routing_positions(ids[:, :2], n_tiles)
        pos3 = pos.reshape(t // MOVE_ROWS, 1, 2 * MOVE_ROWS)
        xs = _dispatch(pos3, h2, jnp.zeros((n_rows, d), F32))
        ys = _experts(tile_expert, n_valid, xs, w_gate[l], w_up[l], w_down[l])
        x2 = _combine(pos3, x1, wts, ys)
    return x2.reshape(batch, seq, d)
```

```python
import functools
import math

import jax
import jax.numpy as jnp
from jax import lax
from jax.experimental import pallas as pl
from jax.experimental.pallas import tpu as pltpu

F32 = jnp.float32
BF16 = jnp.bfloat16

D_MODEL = 1024
HEAD_DIM = 64
GROUP_WIDTH = 512
ROPE_HALF = 8
N_GROUPS = 4
EXPERTS_PER_GROUP = 8
N_EXPERTS = N_GROUPS * EXPERTS_PER_GROUP
EXPERT_FF = 256
RMS_EPS = 1e-6
ROPE_THETA = 500000.0
LANES = 128

ROW_TILE = 512
ATT_BLOCK = 256
SB_Q_ROWS = 512
DA_Q_ROWS = 512
EXPERT_TILE = 256
MOVE_ROWS = 256
VMEM_LIMIT = 48 * 1024 * 1024


def _split_bf16(x):
    hi = x.astype(BF16)
    lo = (x - hi.astype(F32)).astype(BF16)
    return hi, lo


def _inproj_kernel(x_ref, g1_ref, w_ref, qg_ref, kg_ref, cos_ref, sa_ref, sb_ref, seg_ref,
                   daq_ref, dak_ref, dav_ref, sbq_ref, sbk_ref, sbv_ref):
    x = x_ref[...]
    ms = jnp.mean(x * x, axis=-1, keepdims=True)
    h = (x * lax.rsqrt(ms + RMS_EPS) * g1_ref[...]).astype(BF16)

    def proj(c):
        return jnp.dot(h, w_ref[:, c * GROUP_WIDTH:(c + 1) * GROUP_WIDTH],
                       preferred_element_type=F32)

    cos_t = cos_ref[...]
    sa_t = sa_ref[...]
    sb_t = sb_ref[...]
    seg = seg_ref[...]

    def head_norm_rope(p, g):
        hi, lo = _split_bf16(p * p)
        ssum = (jnp.dot(hi, seg, preferred_element_type=F32)
                + jnp.dot(lo, seg, preferred_element_type=F32))
        n = p * lax.rsqrt(ssum * (1.0 / HEAD_DIM) + RMS_EPS) * g
        outs = []
        for c in range(GROUP_WIDTH // LANES):
            t = n[:, c * LANES:(c + 1) * LANES]
            up = pltpu.roll(t, LANES - ROPE_HALF, 1)
            dn = pltpu.roll(t, ROPE_HALF, 1)
            outs.append(t * cos_t + up * sa_t + dn * sb_t)
        return jnp.concatenate(outs, axis=1)

    scale = HEAD_DIM ** -0.5
    daq_ref[...] = (head_norm_rope(proj(0), qg_ref[...]) * scale).astype(BF16)
    dak_ref[...] = head_norm_rope(proj(1), kg_ref[...]).astype(BF16)
    dav_ref[...] = proj(2).astype(BF16)
    sbq_ref[...] = (proj(3) * scale).astype(BF16)
    sbk_ref[...] = proj(4).astype(BF16)
    sbv_ref[...] = proj(5).astype(BF16)


def _inproj(x2, g1, w_bf, qg, kg, cos_t, sa_t, sb_t, seg, seq):
    t = x2.shape[0]
    n_tiles = t // ROW_TILE
    seq_tiles = seq // ROW_TILE
    row = lambda i: (i, 0)
    const = lambda i: (0, 0)
    pos = lambda i: (i % seq_tiles, 0)
    out = jax.ShapeDtypeStruct((t, GROUP_WIDTH), BF16)
    return pl.pallas_call(
        _inproj_kernel,
        out_shape=[out] * 6,
        grid=(n_tiles,),
        in_specs=[
            pl.BlockSpec((ROW_TILE, D_MODEL), row),
            pl.BlockSpec((1, D_MODEL), const),
            pl.BlockSpec((D_MODEL, 6 * GROUP_WIDTH), const),
            pl.BlockSpec((1, GROUP_WIDTH), const),
            pl.BlockSpec((1, GROUP_WIDTH), const),
            pl.BlockSpec((ROW_TILE, LANES), pos),
            pl.BlockSpec((ROW_TILE, LANES), pos),
            pl.BlockSpec((ROW_TILE, LANES), pos),
            pl.BlockSpec((GROUP_WIDTH, GROUP_WIDTH), const),
        ],
        out_specs=[pl.BlockSpec((ROW_TILE, GROUP_WIDTH), row)] * 6,
        compiler_params=pltpu.CompilerParams(
            dimension_semantics=("parallel",), vmem_limit_bytes=VMEM_LIMIT),
        name="inproj",
    )(x2, g1, w_bf, qg, kg, cos_t, sa_t, sb_t, seg)


def _da_kernel(lam_ref, q_ref, k_ref, v_ref, g_ref, o_ref, *, out_scale):
    i = pl.program_id(2)
    blk = ATT_BLOCK
    nsub = DA_Q_ROWS // blk

    def step(j, first_chunk, carry, masked):
        units = [(s, c) for s in range(2) for c in range(first_chunk, nsub)]
        if masked:
            row = lax.broadcasted_iota(jnp.int32, (blk, blk), 0)
            col = lax.broadcasted_iota(jnp.int32, (blk, blk), 1)
        kj = [k_ref[pl.ds(j * blk, blk), s * HEAD_DIM:(s + 1) * HEAD_DIM] for s in range(2)]
        vj = v_ref[pl.ds(j * blk, blk), :]
        sc = {}
        for s, c in units:
            qs = q_ref[c * blk:(c + 1) * blk, s * HEAD_DIM:(s + 1) * HEAD_DIM]
            sc[s, c] = lax.dot_general(qs, kj[s], (((1,), (1,)), ((), ())),
                                       preferred_element_type=F32)
            if masked:
                sc[s, c] = jnp.where(col <= row + (c - first_chunk) * blk, sc[s, c], -jnp.inf)
        m_new = {(s, c): jnp.maximum(carry[s][c][0], jnp.max(sc[s, c], axis=-1, keepdims=True))
                 for s, c in units}
        p = {u: jnp.exp(sc[u] - m_new[u]) for u in units}
        pv = {u: jnp.dot(p[u].astype(BF16), vj, preferred_element_type=F32) for u in units}
        out = []
        for s in range(2):
            chunks = []
            for c in range(nsub):
                if c < first_chunk:
                    chunks.append(carry[s][c])
                    continue
                m, l, acc = carry[s][c]
                alpha = jnp.exp(m - m_new[s, c])
                chunks.append((m_new[s, c],
                               alpha * l + jnp.sum(p[s, c], axis=-1, keepdims=True),
                               alpha * acc + pv[s, c]))
            out.append(tuple(chunks))
        return tuple(out)

    init = tuple(tuple((jnp.full((blk, 1), -jnp.inf, F32), jnp.zeros((blk, 1), F32),
                        jnp.zeros((blk, 2 * HEAD_DIM), F32)) for _ in range(nsub))
                 for _ in range(2))
    carry = lax.fori_loop(0, i * nsub, lambda j, c: step(j, 0, c, False), init)
    for d in range(nsub):
        carry = step(i * nsub + d, d, carry, True)
    lam = lam_ref[0]
    outs = []
    for c in range(nsub):
        (_, l0, a0), (_, l1, a1) = carry[0][c], carry[1][c]
        o = a0 / l0 - lam * (a1 / l1)
        ms = jnp.mean(o * o, axis=-1, keepdims=True)
        outs.append(o * lax.rsqrt(ms + RMS_EPS) * g_ref[...] * out_scale)
    o_ref[...] = jnp.concatenate(outs, axis=0).astype(BF16)


def _diff_attention(lam, q, k, v, g, batch, seq, out_scale):
    nq = seq // DA_Q_ROWS
    heads = GROUP_WIDTH // (2 * HEAD_DIM)
    return pl.pallas_call(
        functools.partial(_da_kernel, out_scale=out_scale),
        out_shape=jax.ShapeDtypeStruct(q.shape, BF16),
        grid=(batch, heads, nq),
        in_specs=[
            pl.BlockSpec(memory_space=pltpu.SMEM),
            pl.BlockSpec((DA_Q_ROWS, 2 * HEAD_DIM), lambda b, h, i: (b * nq + i, h)),
            pl.BlockSpec((seq, 2 * HEAD_DIM), lambda b, h, i: (b, h)),
            pl.BlockSpec((seq, 2 * HEAD_DIM), lambda b, h, i: (b, h)),
            pl.BlockSpec((1, 2 * HEAD_DIM), lambda b, h, i: (0, 0)),
        ],
        out_specs=pl.BlockSpec((DA_Q_ROWS, 2 * HEAD_DIM), lambda b, h, i: (b * nq + i, h)),
        compiler_params=pltpu.CompilerParams(
            dimension_semantics=("parallel", "parallel", "parallel"),
            vmem_limit_bytes=VMEM_LIMIT),
        name="diff_attention",
    )(lam, q, k, v, g)


def _sb_kernel(q_ref, k_ref, v_ref, tri2_ref, g_ref, o_ref):
    i = pl.program_id(2)
    blk = ATT_BLOCK
    nsub = SB_Q_ROWS // blk
    tri2 = tri2_ref[...]

    def step(j, first_chunk, carry, masked):
        units = [(s, c) for s in range(2) for c in range(first_chunk, nsub)]
        if masked:
            row = lax.broadcasted_iota(jnp.int32, (blk, blk), 0)
            col = lax.broadcasted_iota(jnp.int32, (blk, blk), 1)
            offset = lambda c: col < row + (c - first_chunk) * blk
        kj = [k_ref[pl.ds(j * blk, blk), s * HEAD_DIM:(s + 1) * HEAD_DIM] for s in range(2)]
        vj = [v_ref[pl.ds(j * blk, blk), s * HEAD_DIM:(s + 1) * HEAD_DIM] for s in range(2)]
        z = {}
        for s, c in units:
            qs = q_ref[c * blk:(c + 1) * blk, s * HEAD_DIM:(s + 1) * HEAD_DIM]
            z[s, c] = lax.dot_general(qs, kj[s], (((1,), (1,)), ((), ())),
                                      preferred_element_type=F32)
        split = {}
        for u in units:
            sp = jnp.maximum(z[u], 0.0) + jnp.log(1.0 + jnp.exp(-jnp.abs(z[u])))
            if masked:
                sp = jnp.where(offset(u[1]), sp, 0.0)
            split[u] = jnp.concatenate(_split_bf16(sp), axis=1)
        suffix = {u: jnp.dot(split[u], tri2, preferred_element_type=F32) for u in units}
        a = {}
        for s, c in units:
            w = jnp.exp(z[s, c] - suffix[s, c] - carry[s][c][0])
            if masked:
                w = jnp.where(offset(c), w, 0.0)
            a[s, c] = w.astype(BF16)
        pv = {(s, c): jnp.dot(a[s, c], vj[s], preferred_element_type=F32) for s, c in units}
        return tuple(
            tuple(carry[s][c] if c < first_chunk else
                  (carry[s][c][0] + suffix[s, c][:, 0:1], carry[s][c][1] + pv[s, c])
                  for c in range(nsub))
            for s in range(2))

    carry = tuple(tuple((jnp.zeros((blk, 1), F32), jnp.zeros((blk, HEAD_DIM), F32))
                        for _ in range(nsub)) for _ in range(2))
    for d in reversed(range(nsub)):
        carry = step(i * nsub + d, d, carry, True)
    carry = lax.fori_loop(0, i * nsub,
                          lambda t, c: step(i * nsub - 1 - t, 0, c, False), carry)
    g = g_ref[...]
    outs = []
    for s in range(2):
        o = jnp.concatenate([carry[s][c][1] for c in range(nsub)], axis=0)
        ms = jnp.mean(o * o, axis=-1, keepdims=True)
        outs.append(o * lax.rsqrt(ms + RMS_EPS) * g)
    o_ref[...] = jnp.concatenate(outs, axis=1).astype(BF16)


def _stick_breaking_attention(q, k, v, tri, g, batch, seq):
    nq = seq // SB_Q_ROWS
    pairs = GROUP_WIDTH // (2 * HEAD_DIM)
    return pl.pallas_call(
        _sb_kernel,
        out_shape=jax.ShapeDtypeStruct(q.shape, BF16),
        grid=(batch, pairs, nq),
        in_specs=[
            pl.BlockSpec((SB_Q_ROWS, 2 * HEAD_DIM), lambda b, h, i: (b * nq + i, h)),
            pl.BlockSpec((seq, 2 * HEAD_DIM), lambda b, h, i: (b, h)),
            pl.BlockSpec((seq, 2 * HEAD_DIM), lambda b, h, i: (b, h)),
            pl.BlockSpec((2 * ATT_BLOCK, ATT_BLOCK), lambda b, h, i: (0, 0)),
            pl.BlockSpec((1, HEAD_DIM), lambda b, h, i: (0, 0)),
        ],
        out_specs=pl.BlockSpec((SB_Q_ROWS, 2 * HEAD_DIM), lambda b, h, i: (b * nq + i, h)),
        compiler_params=pltpu.CompilerParams(
            dimension_semantics=("parallel", "parallel", "parallel"),
            vmem_limit_bytes=VMEM_LIMIT),
        name="stick_breaking_attention",
    )(q, k, v, tri, g)


def _outproj_kernel(x_ref, oda_ref, osb_ref, w_ref, g2_ref, wrh_ref, wrl_ref, br_ref,
                    x1_ref, h2_ref, ids_ref, wts_ref):
    mix = (jnp.dot(oda_ref[...], w_ref[:GROUP_WIDTH, :], preferred_element_type=F32)
           + jnp.dot(osb_ref[...], w_ref[GROUP_WIDTH:, :], preferred_element_type=F32))
    x1 = x_ref[...] + mix
    x1_ref[...] = x1
    ms = jnp.mean(x1 * x1, axis=-1, keepdims=True)
    h2 = x1 * lax.rsqrt(ms + RMS_EPS) * g2_ref[...]
    h2_ref[...] = h2

    hi, lo = _split_bf16(h2)
    wrh = wrh_ref[...]
    logits = (jnp.dot(hi, wrh, preferred_element_type=F32)
              + jnp.dot(hi, wrl_ref[...], preferred_element_type=F32)
              + jnp.dot(lo, wrh, preferred_element_type=F32)) + br_ref[...]

    rows = logits.shape[0]
    lane = lax.broadcasted_iota(jnp.int32, (rows, LANES), 1).astype(F32)
    neg = -jnp.inf
    gl = jnp.where(lane < N_GROUPS, logits, neg)
    gmax = jnp.max(gl, axis=-1, keepdims=True)
    g_p = 1.0 / jnp.sum(jnp.exp(gl - gmax), axis=-1, keepdims=True)
    g_idx = jnp.min(jnp.where(gl == gmax, lane, float(LANES)), axis=-1, keepdims=True)
    first = N_GROUPS + g_idx * EXPERTS_PER_GROUP
    el = jnp.where((lane >= first) & (lane < first + EXPERTS_PER_GROUP), logits, neg)
    e1 = jnp.max(el, axis=-1, keepdims=True)
    i1 = jnp.min(jnp.where(el == e1, lane, float(LANES)), axis=-1, keepdims=True)
    el2 = jnp.where(lane == i1, neg, el)
    e2 = jnp.max(el2, axis=-1, keepdims=True)
    i2 = jnp.min(jnp.where(el2 == e2, lane, float(LANES)), axis=-1, keepdims=True)
    d = jnp.exp(e2 - e1)
    w1 = g_p / (1.0 + d)
    w2 = g_p * d / (1.0 + d)
    ids = jnp.where(lane == 0.0, i1 - N_GROUPS, jnp.where(lane == 1.0, i2 - N_GROUPS, 0.0))
    ids_ref[...] = ids.astype(jnp.int32)
    wts_ref[...] = jnp.where(lane == 0.0, w1, jnp.where(lane == 1.0, w2, 0.0))


def _outproj(x2, oda, osb, w_bf, g2, wr_hi, wr_lo, br):
    t = x2.shape[0]
    row = lambda i: (i, 0)
    const = lambda i: (0, 0)
    return pl.pallas_call(
        _outproj_kernel,
        out_shape=[
            jax.ShapeDtypeStruct((t, D_MODEL), F32),
            jax.ShapeDtypeStruct((t, D_MODEL), F32),
            jax.ShapeDtypeStruct((t, LANES), jnp.int32),
            jax.ShapeDtypeStruct((t, LANES), F32),
        ],
        grid=(t // ROW_TILE,),
        in_specs=[
            pl.BlockSpec((ROW_TILE, D_MODEL), row),
            pl.BlockSpec((ROW_TILE, GROUP_WIDTH), row),
            pl.BlockSpec((ROW_TILE, GROUP_WIDTH), row),
            pl.BlockSpec((D_MODEL, D_MODEL), const),
            pl.BlockSpec((1, D_MODEL), const),
            pl.BlockSpec((D_MODEL, LANES), const),
            pl.BlockSpec((D_MODEL, LANES), const),
            pl.BlockSpec((1, LANES), const),
        ],
        out_specs=[
            pl.BlockSpec((ROW_TILE, D_MODEL), row),
            pl.BlockSpec((ROW_TILE, D_MODEL), row),
            pl.BlockSpec((ROW_TILE, LANES), row),
            pl.BlockSpec((ROW_TILE, LANES), row),
        ],
        compiler_params=pltpu.CompilerParams(
            dimension_semantics=("parallel",), vmem_limit_bytes=VMEM_LIMIT),
        name="outproj_router",
    )(x2, oda, osb, w_bf, g2, wr_hi, wr_lo, br)


def _row_copy(src, src_row, dst, dst_row, sem):
    return pltpu.make_async_copy(src.at[pl.ds(src_row, 1)], dst.at[pl.ds(dst_row, 1)], sem)


def _dispatch_kernel(pos_ref, h_ref, zero_ref, xs_ref, sem):
    del zero_ref

    def issue(r, c):
        for s in range(2):
            _row_copy(h_ref, r, xs_ref, pos_ref[0, 0, 2 * r + s], sem).start(priority=s)
        return c

    lax.fori_loop(0, MOVE_ROWS, issue, 0, unroll=4)
    for s in range(2):
        pltpu.make_async_copy(h_ref, xs_ref.at[pl.ds(0, MOVE_ROWS)], sem).wait()


def _dispatch(pos3, h2, zeros):
    t = h2.shape[0]
    return pl.pallas_call(
        _dispatch_kernel,
        out_shape=jax.ShapeDtypeStruct(zeros.shape, F32),
        grid=(t // MOVE_ROWS,),
        in_specs=[
            pl.BlockSpec((1, 1, 2 * MOVE_ROWS), lambda i: (i, 0, 0), memory_space=pltpu.SMEM),
            pl.BlockSpec((MOVE_ROWS, D_MODEL), lambda i: (i, 0)),
            pl.BlockSpec(memory_space=pl.ANY),
        ],
        out_specs=pl.BlockSpec(memory_space=pl.ANY),
        scratch_shapes=[pltpu.SemaphoreType.DMA(())],
        input_output_aliases={2: 0},
        compiler_params=pltpu.CompilerParams(
            dimension_semantics=("arbitrary",), has_side_effects=True,
            vmem_limit_bytes=VMEM_LIMIT),
        name="dispatch",
    )(pos3, h2, zeros)


def _experts_kernel(te_ref, nv_ref, xs_ref, wg_ref, wu_ref, wd_ref, ys_ref):
    i = pl.program_id(0)

    @pl.when(i < nv_ref[0])
    def _():
        x = xs_ref[...].astype(BF16)
        hg = jnp.dot(x, wg_ref[0].astype(BF16), preferred_element_type=F32)
        hu = jnp.dot(x, wu_ref[0].astype(BF16), preferred_element_type=F32)
        act = (hg * jax.nn.sigmoid(hg) * hu).astype(BF16)
        ys_ref[...] = jnp.dot(act, wd_ref[0].astype(BF16), preferred_element_type=F32)

    @pl.when(i >= nv_ref[0])
    def _():
        ys_ref[...] = jnp.zeros_like(ys_ref)


def _experts(tile_expert, n_valid, xs, w_gate, w_up, w_down):
    p = xs.shape[0]
    grid_spec = pltpu.PrefetchScalarGridSpec(
        num_scalar_prefetch=2,
        grid=(p // EXPERT_TILE,),
        in_specs=[
            pl.BlockSpec((EXPERT_TILE, D_MODEL), lambda i, te, nv: (i, 0)),
            pl.BlockSpec((1, D_MODEL, EXPERT_FF), lambda i, te, nv: (te[i], 0, 0)),
            pl.BlockSpec((1, D_MODEL, EXPERT_FF), lambda i, te, nv: (te[i], 0, 0)),
            pl.BlockSpec((1, EXPERT_FF, D_MODEL), lambda i, te, nv: (te[i], 0, 0)),
        ],
        out_specs=pl.BlockSpec((EXPERT_TILE, D_MODEL), lambda i, te, nv: (i, 0)),
    )
    return pl.pallas_call(
        _experts_kernel,
        out_shape=jax.ShapeDtypeStruct((p, D_MODEL), F32),
        grid_spec=grid_spec,
        compiler_params=pltpu.CompilerParams(
            dimension_semantics=("arbitrary",), vmem_limit_bytes=VMEM_LIMIT),
        name="experts",
    )(tile_expert, n_valid, xs, w_gate, w_up, w_down)


def _combine_kernel(pos_ref, pos_next_ref, x1_ref, wts_ref, ys_ref, out_ref, buf, sem):
    i = pl.program_id(0)
    n = pl.num_programs(0)

    def gather(p_ref, slot):
        def issue(r, c):
            for s in range(2):
                _row_copy(ys_ref, p_ref[0, 0, 2 * r + s], buf.at[slot, s], r,
                          sem.at[slot]).start(priority=s)
            return c
        lax.fori_loop(0, MOVE_ROWS, issue, 0, unroll=4)

    @pl.when(i == 0)
    def _():
        gather(pos_ref, 0)

    @pl.when(i + 1 < n)
    def _():
        gather(pos_next_ref, (i + 1) % 2)

    slot = i % 2
    for s in range(2):
        pltpu.make_async_copy(ys_ref.at[pl.ds(0, MOVE_ROWS)], buf.at[slot, s], sem.at[slot]).wait()
    w = wts_ref[...]
    out_ref[...] = x1_ref[...] + w[:, 0:1] * buf[slot, 0] + w[:, 1:2] * buf[slot, 1]


def _combine(pos3, x1, wts, ys):
    t = x1.shape[0]
    steps = t // MOVE_ROWS
    return pl.pallas_call(
        _combine_kernel,
        out_shape=jax.ShapeDtypeStruct((t, D_MODEL), F32),
        grid=(steps,),
        in_specs=[
            pl.BlockSpec((1, 1, 2 * MOVE_ROWS), lambda i: (i, 0, 0), memory_space=pltpu.SMEM),
            pl.BlockSpec((1, 1, 2 * MOVE_ROWS), lambda i: (jnp.minimum(i + 1, steps - 1), 0, 0),
                         memory_space=pltpu.SMEM),
            pl.BlockSpec((MOVE_ROWS, D_MODEL), lambda i: (i, 0)),
            pl.BlockSpec((MOVE_ROWS, LANES), lambda i: (i, 0)),
            pl.BlockSpec(memory_space=pl.ANY),
        ],
        out_specs=pl.BlockSpec((MOVE_ROWS, D_MODEL), lambda i: (i, 0)),
        scratch_shapes=[pltpu.VMEM((2, 2, MOVE_ROWS, D_MODEL), F32),
                        pltpu.SemaphoreType.DMA((2,))],
        compiler_params=pltpu.CompilerParams(
            dimension_semantics=("arbitrary",), vmem_limit_bytes=VMEM_LIMIT),
        name="combine",
    )(pos3, pos3, x1, wts, ys)


def _routing_positions(ids, n_tiles):
    e = ids.reshape(-1)
    onehot = (e[:, None] == jnp.arange(N_EXPERTS, dtype=jnp.int32)[None, :]).astype(jnp.int32)
    csum = jnp.cumsum(onehot, axis=0)
    rank = jnp.sum(csum * onehot, axis=1) - 1
    counts = csum[-1]
    padded = ((counts + EXPERT_TILE - 1) // EXPERT_TILE) * EXPERT_TILE
    ends = jnp.cumsum(padded)
    starts = ends - padded
    pos = jnp.sum(starts[None, :] * onehot, axis=1) + rank
    tile_start = jnp.arange(n_tiles, dtype=jnp.int32) * EXPERT_TILE
    tile_expert = jnp.sum((tile_start[:, None] >= ends[None, :]).astype(jnp.int32), axis=1)
    tile_expert = jnp.minimum(tile_expert, N_EXPERTS - 1).astype(jnp.int32)
    n_valid = (ends[-1] // EXPERT_TILE).astype(jnp.int32).reshape(1)
    return pos.astype(jnp.int32), tile_expert, n_valid


def _rope_tables(seq):
    inv_freq = 1.0 / (ROPE_THETA ** (jnp.arange(0, 2 * ROPE_HALF, 2, dtype=F32) / (2 * ROPE_HALF)))
    ang = jnp.arange(seq, dtype=F32)[:, None] * inv_freq[None, :]
    cos, sin = jnp.cos(ang), jnp.sin(ang)
    ones = jnp.ones((seq, HEAD_DIM - 2 * ROPE_HALF), F32)
    zeros8 = jnp.zeros((seq, ROPE_HALF), F32)
    zeros = jnp.zeros((seq, HEAD_DIM - 2 * ROPE_HALF), F32)
    cos_h = jnp.concatenate([cos, cos, ones], axis=1)
    sa_h = jnp.concatenate([-sin, zeros8, zeros], axis=1)
    sb_h = jnp.concatenate([zeros8, sin, zeros], axis=1)
    rep = LANES // HEAD_DIM
    return jnp.tile(cos_h, (1, rep)), jnp.tile(sa_h, (1, rep)), jnp.tile(sb_h, (1, rep))


def kernel(x, norm1_g, w_in, q_norm_g, k_norm_g, lambda_q1, lambda_k1, lambda_q2, lambda_k2,
           da_out_g, sb_out_g, w_out, norm2_g, w_router_group, b_router_group,
           w_router_expert, b_router_expert, w_gate, w_up, w_down):
    batch, seq, d = x.shape
    depth = w_in.shape[0]
    t = batch * seq
    assert d == D_MODEL and seq % ROW_TILE == 0 and seq % ATT_BLOCK == 0 and t % MOVE_ROWS == 0

    cos_t, sa_t, sb_t = _rope_tables(seq)
    head_of = jnp.arange(GROUP_WIDTH) // HEAD_DIM
    seg = (head_of[:, None] == head_of[None, :]).astype(BF16)
    kk = jnp.arange(ATT_BLOCK)
    tri = (kk[:, None] >= kk[None, :]).astype(BF16)
    tri = jnp.concatenate([tri, tri], axis=0)
    n_rows = 2 * t + N_EXPERTS * EXPERT_TILE
    n_tiles = n_rows // EXPERT_TILE
    heads_per_group = GROUP_WIDTH // HEAD_DIM

    x2 = x.reshape(t, d)
    for l in range(depth):
        lambda_init = 0.8 - 0.6 * math.exp(-0.3 * l)
        lam = (jnp.exp(jnp.sum(lambda_q1[l] * lambda_k1[l]))
               - jnp.exp(jnp.sum(lambda_q2[l] * lambda_k2[l])) + lambda_init).reshape(1).astype(F32)

        daq, dak, dav, sbq, sbk, sbv = _inproj(
            x2, norm1_g[l][None, :], w_in[l].astype(BF16),
            jnp.tile(q_norm_g[l], heads_per_group)[None, :],
            jnp.tile(k_norm_g[l], heads_per_group)[None, :],
            cos_t, sa_t, sb_t, seg, seq)

        o_da = _diff_attention(lam, daq, dak, dav, da_out_g[l][None, :], batch, seq,
                               1.0 - lambda_init)
        o_sb = _stick_breaking_attention(sbq, sbk, sbv, tri, sb_out_g[l][None, :], batch, seq)

        wr = jnp.concatenate([w_router_group[l], w_router_expert[l]], axis=1)
        wr = jnp.pad(wr, ((0, 0), (0, LANES - wr.shape[1])))
        wr_hi, wr_lo = _split_bf16(wr)
        br = jnp.concatenate([b_router_group[l], b_router_expert[l]])
        br = jnp.pad(br, (0, LANES - br.shape[0]))[None, :]
        x1, h2, ids, wts = _outproj(x2, o_da, o_sb, w_out[l].astype(BF16), norm2_g[l][None, :],
                                    wr_hi, wr_lo, br)

        pos, tile_expert, n_valid = _routing_positions(ids[:, :2], n_tiles)
        pos3 = pos.reshape(t // MOVE_ROWS, 1, 2 * MOVE_ROWS)
        xs = _dispatch(pos3, h2, jnp.zeros((n_rows, d), F32))
        ys = _experts(tile_expert, n_valid, xs, w_gate[l], w_up[l], w_down[l])
        x2 = _combine(pos3, x1, wts, ys)
    return x2.reshape(batch, seq, d)
```

```python
import functools
import math

import jax
import jax.numpy as jnp
from jax import lax
from jax.experimental import pallas as pl
from jax.experimental.pallas import tpu as pltpu

F32 = jnp.float32
BF16 = jnp.bfloat16

D_MODEL = 1024
HEAD_DIM = 64
GROUP_WIDTH = 512
ROPE_HALF = 8
N_GROUPS = 4
EXPERTS_PER_GROUP = 8
N_EXPERTS = N_GROUPS * EXPERTS_PER_GROUP
EXPERT_FF = 256
RMS_EPS = 1e-6
ROPE_THETA = 500000.0
LANES = 128

ROW_TILE = 512
ATT_BLOCK = 256
SB_Q_ROWS = 512
DA_Q_ROWS = 512
DA_KEY_SPAN = 512
EXPERT_TILE = 256
MOVE_ROWS = 256
VMEM_LIMIT = 48 * 1024 * 1024


def _split_bf16(x):
    hi = x.astype(BF16)
    lo = (x - hi.astype(F32)).astype(BF16)
    return hi, lo


def _inproj_kernel(x_ref, g1_ref, w_ref, qg_ref, kg_ref, cos_ref, sa_ref, sb_ref, seg_ref,
                   daq_ref, dak_ref, dav_ref, sbq_ref, sbk_ref, sbv_ref):
    x = x_ref[...]
    ms = jnp.mean(x * x, axis=-1, keepdims=True)
    h = (x * lax.rsqrt(ms + RMS_EPS) * g1_ref[...]).astype(BF16)

    def proj(c):
        return jnp.dot(h, w_ref[:, c * GROUP_WIDTH:(c + 1) * GROUP_WIDTH],
                       preferred_element_type=F32)

    cos_t = cos_ref[...]
    sa_t = sa_ref[...]
    sb_t = sb_ref[...]
    seg = seg_ref[...]

    def head_norm_rope(p, g):
        hi, lo = _split_bf16(p * p)
        ssum = (jnp.dot(hi, seg, preferred_element_type=F32)
                + jnp.dot(lo, seg, preferred_element_type=F32))
        n = p * lax.rsqrt(ssum * (1.0 / HEAD_DIM) + RMS_EPS) * g
        outs = []
        for c in range(GROUP_WIDTH // LANES):
            t = n[:, c * LANES:(c + 1) * LANES]
            up = pltpu.roll(t, LANES - ROPE_HALF, 1)
            dn = pltpu.roll(t, ROPE_HALF, 1)
            outs.append(t * cos_t + up * sa_t + dn * sb_t)
        return jnp.concatenate(outs, axis=1)

    scale = HEAD_DIM ** -0.5
    daq_ref[...] = (head_norm_rope(proj(0), qg_ref[...]) * scale).astype(BF16)
    dak_ref[...] = head_norm_rope(proj(1), kg_ref[...]).astype(BF16)
    dav_ref[...] = proj(2).astype(BF16)
    sbq_ref[...] = (proj(3) * scale).astype(BF16)
    sbk_ref[...] = proj(4).astype(BF16)
    sbv_ref[...] = proj(5).astype(BF16)


def _inproj(x2, g1, w_bf, qg, kg, cos_t, sa_t, sb_t, seg, seq):
    t = x2.shape[0]
    n_tiles = t // ROW_TILE
    seq_tiles = seq // ROW_TILE
    row = lambda i: (i, 0)
    const = lambda i: (0, 0)
    pos = lambda i: (i % seq_tiles, 0)
    out = jax.ShapeDtypeStruct((t, GROUP_WIDTH), BF16)
    return pl.pallas_call(
        _inproj_kernel,
        out_shape=[out] * 6,
        grid=(n_tiles,),
        in_specs=[
            pl.BlockSpec((ROW_TILE, D_MODEL), row),
            pl.BlockSpec((1, D_MODEL), const),
            pl.BlockSpec((D_MODEL, 6 * GROUP_WIDTH), const),
            pl.BlockSpec((1, GROUP_WIDTH), const),
            pl.BlockSpec((1, GROUP_WIDTH), const),
            pl.BlockSpec((ROW_TILE, LANES), pos),
            pl.BlockSpec((ROW_TILE, LANES), pos),
            pl.BlockSpec((ROW_TILE, LANES), pos),
            pl.BlockSpec((GROUP_WIDTH, GROUP_WIDTH), const),
        ],
        out_specs=[pl.BlockSpec((ROW_TILE, GROUP_WIDTH), row)] * 6,
        compiler_params=pltpu.CompilerParams(
            dimension_semantics=("parallel",), vmem_limit_bytes=VMEM_LIMIT),
        name="inproj",
    )(x2, g1, w_bf, qg, kg, cos_t, sa_t, sb_t, seg)


def _da_kernel(lam_ref, q_ref, k_ref, v_ref, g_ref, o_ref, m_ref, l_ref, acc_ref, *, out_scale):
    i = pl.program_id(2)
    blk = ATT_BLOCK
    nsub = DA_Q_ROWS // blk
    m_ref[...] = jnp.full_like(m_ref, -jnp.inf)
    l_ref[...] = jnp.zeros_like(l_ref)
    acc_ref[...] = jnp.zeros_like(acc_ref)

    def step(key0, span, first_chunk, masked):
        units = [(s, c) for s in range(2) for c in range(first_chunk, nsub)]
        if masked:
            row = lax.broadcasted_iota(jnp.int32, (blk, span), 0)
            col = lax.broadcasted_iota(jnp.int32, (blk, span), 1)
        kj = [k_ref[pl.ds(key0, span), s * HEAD_DIM:(s + 1) * HEAD_DIM] for s in range(2)]
        vj = v_ref[pl.ds(key0, span), :]
        sc = {}
        for s, c in units:
            qs = q_ref[c * blk:(c + 1) * blk, s * HEAD_DIM:(s + 1) * HEAD_DIM]
            sc[s, c] = lax.dot_general(qs, kj[s], (((1,), (1,)), ((), ())),
                                       preferred_element_type=F32)
            if masked:
                sc[s, c] = jnp.where(col <= row + (c - first_chunk) * blk, sc[s, c], -jnp.inf)
        wide = (blk, LANES)
        m_new = {}
        p = {}
        for u in units:
            parts = [sc[u][:, x:x + LANES] for x in range(0, span, LANES)]
            row_max = jnp.max(functools.reduce(jnp.maximum, parts), axis=-1, keepdims=True)
            m_new[u] = jnp.maximum(m_ref[u], jnp.broadcast_to(row_max, wide))
            p[u] = [jnp.exp(x - m_new[u]) for x in parts]
        pv = {u: jnp.dot(jnp.concatenate(p[u], axis=1).astype(BF16), vj,
                         preferred_element_type=F32) for u in units}
        for u in units:
            alpha = jnp.exp(m_ref[u] - m_new[u])
            m_ref[u] = m_new[u]
            l_ref[u] = alpha * l_ref[u] + functools.reduce(jnp.add, p[u])
            acc_ref[u] = alpha * acc_ref[u] + pv[u]

    def body(j, c):
        step(pl.multiple_of(j * DA_KEY_SPAN, DA_KEY_SPAN), DA_KEY_SPAN, 0, False)
        return c

    lax.fori_loop(0, i * (DA_Q_ROWS // DA_KEY_SPAN), body, 0)
    for d in range(nsub):
        step((i * nsub + d) * blk, blk, d, True)
    lam = lam_ref[0]
    outs = []
    for c in range(nsub):
        l0 = jnp.sum(l_ref[0, c], axis=-1, keepdims=True)
        l1 = jnp.sum(l_ref[1, c], axis=-1, keepdims=True)
        o = acc_ref[0, c] / l0 - lam * (acc_ref[1, c] / l1)
        ms = jnp.mean(o * o, axis=-1, keepdims=True)
        outs.append(o * lax.rsqrt(ms + RMS_EPS) * g_ref[...] * out_scale)
    o_ref[...] = jnp.concatenate(outs, axis=0).astype(BF16)


def _diff_attention(lam, q, k, v, g, batch, seq, out_scale):
    nq = seq // DA_Q_ROWS
    heads = GROUP_WIDTH // (2 * HEAD_DIM)
    return pl.pallas_call(
        functools.partial(_da_kernel, out_scale=out_scale),
        out_shape=jax.ShapeDtypeStruct(q.shape, BF16),
        grid=(batch, heads, nq),
        in_specs=[
            pl.BlockSpec(memory_space=pltpu.SMEM),
            pl.BlockSpec((DA_Q_ROWS, 2 * HEAD_DIM), lambda b, h, i: (b * nq + i, h)),
            pl.BlockSpec((seq, 2 * HEAD_DIM), lambda b, h, i: (b, h)),
            pl.BlockSpec((seq, 2 * HEAD_DIM), lambda b, h, i: (b, h)),
            pl.BlockSpec((1, 2 * HEAD_DIM), lambda b, h, i: (0, 0)),
        ],
        out_specs=pl.BlockSpec((DA_Q_ROWS, 2 * HEAD_DIM), lambda b, h, i: (b * nq + i, h)),
        scratch_shapes=[
            pltpu.VMEM((2, DA_Q_ROWS // ATT_BLOCK, ATT_BLOCK, LANES), F32),
            pltpu.VMEM((2, DA_Q_ROWS // ATT_BLOCK, ATT_BLOCK, LANES), F32),
            pltpu.VMEM((2, DA_Q_ROWS // ATT_BLOCK, ATT_BLOCK, 2 * HEAD_DIM), F32),
        ],
        compiler_params=pltpu.CompilerParams(
            dimension_semantics=("parallel", "parallel", "parallel"),
            vmem_limit_bytes=VMEM_LIMIT),
        name="diff_attention",
    )(lam, q, k, v, g)


def _sb_kernel(q_ref, k_ref, v_ref, tri2_ref, g_ref, o_ref, tail_ref, acc_ref):
    i = pl.program_id(2)
    blk = ATT_BLOCK
    nsub = SB_Q_ROWS // blk
    tri2 = tri2_ref[...]
    tail_ref[...] = jnp.zeros_like(tail_ref)
    acc_ref[...] = jnp.zeros_like(acc_ref)

    def step(j, first_chunk, masked):
        units = [(s, c) for s in range(2) for c in range(first_chunk, nsub)]
        if masked:
            row = lax.broadcasted_iota(jnp.int32, (blk, blk), 0)
            col = lax.broadcasted_iota(jnp.int32, (blk, blk), 1)
            offset = lambda c: col < row + (c - first_chunk) * blk
        kj = [k_ref[pl.ds(j * blk, blk), s * HEAD_DIM:(s + 1) * HEAD_DIM] for s in range(2)]
        vj = [v_ref[pl.ds(j * blk, blk), s * HEAD_DIM:(s + 1) * HEAD_DIM] for s in range(2)]
        z = {}
        for s, c in units:
            qs = q_ref[c * blk:(c + 1) * blk, s * HEAD_DIM:(s + 1) * HEAD_DIM]
            z[s, c] = lax.dot_general(qs, kj[s], (((1,), (1,)), ((), ())),
                                      preferred_element_type=F32)
        split = {}
        for u in units:
            sp = jnp.maximum(z[u], 0.0) + jnp.log(1.0 + jnp.exp(-jnp.abs(z[u])))
            if masked:
                sp = jnp.where(offset(u[1]), sp, 0.0)
            split[u] = jnp.concatenate(_split_bf16(sp), axis=1)
        suffix = {u: jnp.dot(split[u], tri2, preferred_element_type=F32) for u in units}
        a = {}
        for s, c in units:
            w = jnp.exp(z[s, c] - suffix[s, c] - tail_ref[s, c])
            if masked:
                w = jnp.where(offset(c), w, 0.0)
            a[s, c] = w.astype(BF16)
        for s, c in units:
            tail_ref[s, c] += suffix[s, c][:, 0:1]
            acc_ref[s, c] += jnp.dot(a[s, c], vj[s], preferred_element_type=F32)

    for d in reversed(range(nsub)):
        step(i * nsub + d, d, True)

    def body(t, c):
        for d in range(nsub):
            step((i - t) * nsub - 1 - d, 0, False)
        return c

    lax.fori_loop(0, i, body, 0)
    g = g_ref[...]
    outs = []
    for s in range(2):
        o = jnp.concatenate([acc_ref[s, c] for c in range(nsub)], axis=0)
        ms = jnp.mean(o * o, axis=-1, keepdims=True)
        outs.append(o * lax.rsqrt(ms + RMS_EPS) * g)
    o_ref[...] = jnp.concatenate(outs, axis=1).astype(BF16)


def _stick_breaking_attention(q, k, v, tri, g, batch, seq):
    nq = seq // SB_Q_ROWS
    pairs = GROUP_WIDTH // (2 * HEAD_DIM)
    return pl.pallas_call(
        _sb_kernel,
        out_shape=jax.ShapeDtypeStruct(q.shape, BF16),
        grid=(batch, pairs, nq),
        in_specs=[
            pl.BlockSpec((SB_Q_ROWS, 2 * HEAD_DIM), lambda b, h, i: (b * nq + i, h)),
            pl.BlockSpec((seq, 2 * HEAD_DIM), lambda b, h, i: (b, h)),
            pl.BlockSpec((seq, 2 * HEAD_DIM), lambda b, h, i: (b, h)),
            pl.BlockSpec((2 * ATT_BLOCK, ATT_BLOCK), lambda b, h, i: (0, 0)),
            pl.BlockSpec((1, HEAD_DIM), lambda b, h, i: (0, 0)),
        ],
        out_specs=pl.BlockSpec((SB_Q_ROWS, 2 * HEAD_DIM), lambda b, h, i: (b * nq + i, h)),
        scratch_shapes=[
            pltpu.VMEM((2, SB_Q_ROWS // ATT_BLOCK, ATT_BLOCK, 1), F32),
            pltpu.VMEM((2, SB_Q_ROWS // ATT_BLOCK, ATT_BLOCK, HEAD_DIM), F32),
        ],
        compiler_params=pltpu.CompilerParams(
            dimension_semantics=("parallel", "parallel", "parallel"),
            vmem_limit_bytes=VMEM_LIMIT),
        name="stick_breaking_attention",
    )(q, k, v, tri, g)


def _outproj_kernel(x_ref, oda_ref, osb_ref, w_ref, g2_ref, wrh_ref, wrl_ref, br_ref,
                    x1_ref, h2_ref, ids_ref, wts_ref):
    mix = (jnp.dot(oda_ref[...], w_ref[:GROUP_WIDTH, :], preferred_element_type=F32)
           + jnp.dot(osb_ref[...], w_ref[GROUP_WIDTH:, :], preferred_element_type=F32))
    x1 = x_ref[...] + mix
    x1_ref[...] = x1
    ms = jnp.mean(x1 * x1, axis=-1, keepdims=True)
    h2 = x1 * lax.rsqrt(ms + RMS_EPS) * g2_ref[...]
    h2_ref[...] = h2

    hi, lo = _split_bf16(h2)
    wrh = wrh_ref[...]
    logits = (jnp.dot(hi, wrh, preferred_element_type=F32)
              + jnp.dot(hi, wrl_ref[...], preferred_element_type=F32)
              + jnp.dot(lo, wrh, preferred_element_type=F32)) + br_ref[...]

    rows = logits.shape[0]
    lane = lax.broadcasted_iota(jnp.int32, (rows, LANES), 1).astype(F32)
    neg = -jnp.inf
    gl = jnp.where(lane < N_GROUPS, logits, neg)
    gmax = jnp.max(gl, axis=-1, keepdims=True)
    g_p = 1.0 / jnp.sum(jnp.exp(gl - gmax), axis=-1, keepdims=True)
    g_idx = jnp.min(jnp.where(gl == gmax, lane, float(LANES)), axis=-1, keepdims=True)
    first = N_GROUPS + g_idx * EXPERTS_PER_GROUP
    el = jnp.where((lane >= first) & (lane < first + EXPERTS_PER_GROUP), logits, neg)
    e1 = jnp.max(el, axis=-1, keepdims=True)
    i1 = jnp.min(jnp.where(el == e1, lane, float(LANES)), axis=-1, keepdims=True)
    el2 = jnp.where(lane == i1, neg, el)
    e2 = jnp.max(el2, axis=-1, keepdims=True)
    i2 = jnp.min(jnp.where(el2 == e2, lane, float(LANES)), axis=-1, keepdims=True)
    d = jnp.exp(e2 - e1)
    w1 = g_p / (1.0 + d)
    w2 = g_p * d / (1.0 + d)
    ids = jnp.where(lane == 0.0, i1 - N_GROUPS, jnp.where(lane == 1.0, i2 - N_GROUPS, 0.0))
    ids_ref[...] = ids.astype(jnp.int32)
    wts_ref[...] = jnp.where(lane == 0.0, w1, jnp.where(lane == 1.0, w2, 0.0))


def _outproj(x2, oda, osb, w_bf, g2, wr_hi, wr_lo, br):
    t = x2.shape[0]
    row = lambda i: (i, 0)
    const = lambda i: (0, 0)
    return pl.pallas_call(
        _outproj_kernel,
        out_shape=[
            jax.ShapeDtypeStruct((t, D_MODEL), F32),
            jax.ShapeDtypeStruct((t, D_MODEL), F32),
            jax.ShapeDtypeStruct((t, LANES), jnp.int32),
            jax.ShapeDtypeStruct((t, LANES), F32),
        ],
        grid=(t // ROW_TILE,),
        in_specs=[
            pl.BlockSpec((ROW_TILE, D_MODEL), row),
            pl.BlockSpec((ROW_TILE, GROUP_WIDTH), row),
            pl.BlockSpec((ROW_TILE, GROUP_WIDTH), row),
            pl.BlockSpec((D_MODEL, D_MODEL), const),
            pl.BlockSpec((1, D_MODEL), const),
            pl.BlockSpec((D_MODEL, LANES), const),
            pl.BlockSpec((D_MODEL, LANES), const),
            pl.BlockSpec((1, LANES), const),
        ],
        out_specs=[
            pl.BlockSpec((ROW_TILE, D_MODEL), row),
            pl.BlockSpec((ROW_TILE, D_MODEL), row),
            pl.BlockSpec((ROW_TILE, LANES), row),
            pl.BlockSpec((ROW_TILE, LANES), row),
        ],
        compiler_params=pltpu.CompilerParams(
            dimension_semantics=("parallel",), vmem_limit_bytes=VMEM_LIMIT),
        name="outproj_router",
    )(x2, oda, osb, w_bf, g2, wr_hi, wr_lo, br)


def _row_copy(src, src_row, dst, dst_row, sem):
    return pltpu.make_async_copy(src.at[pl.ds(src_row, 1)], dst.at[pl.ds(dst_row, 1)], sem)


def _dispatch_kernel(pos_ref, h_ref, zero_ref, xs_ref, sem):
    del zero_ref

    def issue(r, c):
        for s in range(2):
            _row_copy(h_ref, r, xs_ref, pos_ref[0, 0, 2 * r + s], sem).start(priority=s)
        return c

    lax.fori_loop(0, MOVE_ROWS, issue, 0, unroll=4)
    for s in range(2):
        pltpu.make_async_copy(h_ref, xs_ref.at[pl.ds(0, MOVE_ROWS)], sem).wait()


def _dispatch(pos3, h2, zeros):
    t = h2.shape[0]
    return pl.pallas_call(
        _dispatch_kernel,
        out_shape=jax.ShapeDtypeStruct(zeros.shape, F32),
        grid=(t // MOVE_ROWS,),
        in_specs=[
            pl.BlockSpec((1, 1, 2 * MOVE_ROWS), lambda i: (i, 0, 0), memory_space=pltpu.SMEM),
            pl.BlockSpec((MOVE_ROWS, D_MODEL), lambda i: (i, 0)),
            pl.BlockSpec(memory_space=pl.ANY),
        ],
        out_specs=pl.BlockSpec(memory_space=pl.ANY),
        scratch_shapes=[pltpu.SemaphoreType.DMA(())],
        input_output_aliases={2: 0},
        compiler_params=pltpu.CompilerParams(
            dimension_semantics=("arbitrary",), has_side_effects=True,
            vmem_limit_bytes=VMEM_LIMIT),
        name="dispatch",
    )(pos3, h2, zeros)


def _experts_kernel(te_ref, nv_ref, xs_ref, wg_ref, wu_ref, wd_ref, ys_ref):
    i = pl.program_id(0)

    @pl.when(i < nv_ref[0])
    def _():
        x = xs_ref[...].astype(BF16)
        hg = jnp.dot(x, wg_ref[0, 0].astype(BF16), preferred_element_type=F32)
        hu = jnp.dot(x, wu_ref[0, 0].astype(BF16), preferred_element_type=F32)
        act = (hg * jax.nn.sigmoid(hg) * hu).astype(BF16)
        ys_ref[...] = jnp.dot(act, wd_ref[0, 0].astype(BF16), preferred_element_type=F32)

    @pl.when(i >= nv_ref[0])
    def _():
        ys_ref[...] = jnp.zeros_like(ys_ref)


def _experts(tile_expert, n_valid, xs, w_gate, w_up, w_down, layer):
    p = xs.shape[0]
    grid_spec = pltpu.PrefetchScalarGridSpec(
        num_scalar_prefetch=2,
        grid=(p // EXPERT_TILE,),
        in_specs=[
            pl.BlockSpec((EXPERT_TILE, D_MODEL), lambda i, te, nv: (i, 0)),
            pl.BlockSpec((1, 1, D_MODEL, EXPERT_FF), lambda i, te, nv: (layer, te[i], 0, 0)),
            pl.BlockSpec((1, 1, D_MODEL, EXPERT_FF), lambda i, te, nv: (layer, te[i], 0, 0)),
            pl.BlockSpec((1, 1, EXPERT_FF, D_MODEL), lambda i, te, nv: (layer, te[i], 0, 0)),
        ],
        out_specs=pl.BlockSpec((EXPERT_TILE, D_MODEL), lambda i, te, nv: (i, 0)),
    )
    return pl.pallas_call(
        _experts_kernel,
        out_shape=jax.ShapeDtypeStruct((p, D_MODEL), F32),
        grid_spec=grid_spec,
        compiler_params=pltpu.CompilerParams(
            dimension_semantics=("arbitrary",), vmem_limit_bytes=VMEM_LIMIT),
        name="experts",
    )(tile_expert, n_valid, xs, w_gate, w_up, w_down)


def _combine_kernel(pos_ref, pos_next_ref, x1_ref, wts_ref, ys_ref, out_ref, buf, sem):
    i = pl.program_id(0)
    n = pl.num_programs(0)

    def gather(p_ref, slot):
        def issue(r, c):
            for s in range(2):
                _row_copy(ys_ref, p_ref[0, 0, 2 * r + s], buf.at[slot, s], r,
                          sem.at[slot]).start(priority=s)
            return c
        lax.fori_loop(0, MOVE_ROWS, issue, 0, unroll=4)

    @pl.when(i == 0)
    def _():
        gather(pos_ref, 0)

    @pl.when(i + 1 < n)
    def _():
        gather(pos_next_ref, (i + 1) % 2)

    slot = i % 2
    for s in range(2):
        pltpu.make_async_copy(ys_ref.at[pl.ds(0, MOVE_ROWS)], buf.at[slot, s], sem.at[slot]).wait()
    w = wts_ref[...]
    out_ref[...] = x1_ref[...] + w[:, 0:1] * buf[slot, 0] + w[:, 1:2] * buf[slot, 1]


def _combine(pos3, x1, wts, ys):
    t = x1.shape[0]
    steps = t // MOVE_ROWS
    return pl.pallas_call(
        _combine_kernel,
        out_shape=jax.ShapeDtypeStruct((t, D_MODEL), F32),
        grid=(steps,),
        in_specs=[
            pl.BlockSpec((1, 1, 2 * MOVE_ROWS), lambda i: (i, 0, 0), memory_space=pltpu.SMEM),
            pl.BlockSpec((1, 1, 2 * MOVE_ROWS), lambda i: (jnp.minimum(i + 1, steps - 1), 0, 0),
                         memory_space=pltpu.SMEM),
            pl.BlockSpec((MOVE_ROWS, D_MODEL), lambda i: (i, 0)),
            pl.BlockSpec((MOVE_ROWS, LANES), lambda i: (i, 0)),
            pl.BlockSpec(memory_space=pl.ANY),
        ],
        out_specs=pl.BlockSpec((MOVE_ROWS, D_MODEL), lambda i: (i, 0)),
        scratch_shapes=[pltpu.VMEM((2, 2, MOVE_ROWS, D_MODEL), F32),
                        pltpu.SemaphoreType.DMA((2,))],
        compiler_params=pltpu.CompilerParams(
            dimension_semantics=("arbitrary",), vmem_limit_bytes=VMEM_LIMIT),
        name="combine",
    )(pos3, pos3, x1, wts, ys)


def _routing_positions(ids, n_tiles):
    e = ids.reshape(-1)
    onehot = (e[:, None] == jnp.arange(N_EXPERTS, dtype=jnp.int32)[None, :]).astype(jnp.int32)
    csum = jnp.cumsum(onehot, axis=0)
    rank = jnp.sum(csum * onehot, axis=1) - 1
    counts = csum[-1]
    padded = ((counts + EXPERT_TILE - 1) // EXPERT_TILE) * EXPERT_TILE
    ends = jnp.cumsum(padded)
    starts = ends - padded
    pos = jnp.sum(starts[None, :] * onehot, axis=1) + rank
    tile_start = jnp.arange(n_tiles, dtype=jnp.int32) * EXPERT_TILE
    tile_expert = jnp.sum((tile_start[:, None] >= ends[None, :]).astype(jnp.int32), axis=1)
    tile_expert = jnp.minimum(tile_expert, N_EXPERTS - 1).astype(jnp.int32)
    n_valid = (ends[-1] // EXPERT_TILE).astype(jnp.int32).reshape(1)
    return pos.astype(jnp.int32), tile_expert, n_valid


def _rope_tables(seq):
    inv_freq = 1.0 / (ROPE_THETA ** (jnp.arange(0, 2 * ROPE_HALF, 2, dtype=F32) / (2 * ROPE_HALF)))
    ang = jnp.arange(seq, dtype=F32)[:, None] * inv_freq[None, :]
    cos, sin = jnp.cos(ang), jnp.sin(ang)
    ones = jnp.ones((seq, HEAD_DIM - 2 * ROPE_HALF), F32)
    zeros8 = jnp.zeros((seq, ROPE_HALF), F32)
    zeros = jnp.zeros((seq, HEAD_DIM - 2 * ROPE_HALF), F32)
    cos_h = jnp.concatenate([cos, cos, ones], axis=1)
    sa_h = jnp.concatenate([-sin, zeros8, zeros], axis=1)
    sb_h = jnp.concatenate([zeros8, sin, zeros], axis=1)
    rep = LANES // HEAD_DIM
    return jnp.tile(cos_h, (1, rep)), jnp.tile(sa_h, (1, rep)), jnp.tile(sb_h, (1, rep))


def kernel(x, norm1_g, w_in, q_norm_g, k_norm_g, lambda_q1, lambda_k1, lambda_q2, lambda_k2,
           da_out_g, sb_out_g, w_out, norm2_g, w_router_group, b_router_group,
           w_router_expert, b_router_expert, w_gate, w_up, w_down):
    batch, seq, d = x.shape
    depth = w_in.shape[0]
    t = batch * seq
    assert d == D_MODEL and seq % ROW_TILE == 0 and seq % ATT_BLOCK == 0 and t % MOVE_ROWS == 0

    cos_t, sa_t, sb_t = _rope_tables(seq)
    head_of = jnp.arange(GROUP_WIDTH) // HEAD_DIM
    seg = (head_of[:, None] == head_of[None, :]).astype(BF16)
    kk = jnp.arange(ATT_BLOCK)
    tri = (kk[:, None] >= kk[None, :]).astype(BF16)
    tri = jnp.concatenate([tri, tri], axis=0)
    n_rows = 2 * t + N_EXPERTS * EXPERT_TILE
    n_tiles = n_rows // EXPERT_TILE
    heads_per_group = GROUP_WIDTH // HEAD_DIM

    x2 = x.reshape(t, d)
    xs = jnp.zeros((n_rows, d), F32)
    for l in range(depth):
        lambda_init = 0.8 - 0.6 * math.exp(-0.3 * l)
        lam = (jnp.exp(jnp.sum(lambda_q1[l] * lambda_k1[l]))
               - jnp.exp(jnp.sum(lambda_q2[l] * lambda_k2[l])) + lambda_init).reshape(1).astype(F32)

        daq, dak, dav, sbq, sbk, sbv = _inproj(
            x2, norm1_g[l][None, :], w_in[l].astype(BF16),
            jnp.tile(q_norm_g[l], heads_per_group)[None, :],
            jnp.tile(k_norm_g[l], heads_per_group)[None, :],
            cos_t, sa_t, sb_t, seg, seq)

        o_da = _diff_attention(lam, daq, dak, dav, da_out_g[l][None, :], batch, seq,
                               1.0 - lambda_init)
        o_sb = _stick_breaking_attention(sbq, sbk, sbv, tri, sb_out_g[l][None, :], batch, seq)

        wr = jnp.concatenate([w_router_group[l], w_router_expert[l]], axis=1)
        wr = jnp.pad(wr, ((0, 0), (0, LANES - wr.shape[1])))
        wr_hi, wr_lo = _split_bf16(wr)
        br = jnp.concatenate([b_router_group[l], b_router_expert[l]])
        br = jnp.pad(br, (0, LANES - br.shape[0]))[None, :]
        x1, h2, ids, wts = _outproj(x2, o_da, o_sb, w_out[l].astype(BF16), norm2_g[l][None, :],
                                    wr_hi, wr_lo, br)

        pos, tile_expert, n_valid = _routing_positions(ids[:, :2], n_tiles)
        pos3 = pos.reshape(t // MOVE_ROWS, 1, 2 * MOVE_ROWS)
        xs = _dispatch(pos3, h2, xs)
        ys = _experts(tile_expert, n_valid, xs, w_gate, w_up, w_down, l)
        x2 = _combine(pos3, x1, wts, ys)
    return x2.reshape(batch, seq, d)
```

```python
import functools
import math

import jax
import jax.numpy as jnp
from jax import lax
from jax.experimental import pallas as pl
from jax.experimental.pallas import tpu as pltpu

F32 = jnp.float32
BF16 = jnp.bfloat16

D_MODEL = 1024
HEAD_DIM = 64
GROUP_WIDTH = 512
ROPE_HALF = 8
N_GROUPS = 4
EXPERTS_PER_GROUP = 8
N_EXPERTS = N_GROUPS * EXPERTS_PER_GROUP
EXPERT_FF = 256
RMS_EPS = 1e-6
ROPE_THETA = 500000.0
LANES = 128

ROW_TILE = 512
ATT_BLOCK = 256
SB_Q_ROWS = 1024
DA_Q_ROWS = 1024
DA_KEY_SPAN = 1024
EXPERT_TILE = 256
MOVE_ROWS = 256
VMEM_LIMIT = 48 * 1024 * 1024


def _split_bf16(x):
    hi = x.astype(BF16)
    lo = (x - hi.astype(F32)).astype(BF16)
    return hi, lo


def _inproj_kernel(x_ref, g1_ref, w_ref, qg_ref, kg_ref, cos_ref, sa_ref, sb_ref, seg_ref,
                   daq_ref, dak_ref, dav_ref, sbq_ref, sbk_ref, sbv_ref):
    x = x_ref[...]
    ms = jnp.mean(x * x, axis=-1, keepdims=True)
    h = (x * lax.rsqrt(ms + RMS_EPS) * g1_ref[...]).astype(BF16)

    def proj(c):
        return jnp.dot(h, w_ref[:, c * GROUP_WIDTH:(c + 1) * GROUP_WIDTH],
                       preferred_element_type=F32)

    cos_t = cos_ref[...]
    sa_t = sa_ref[...]
    sb_t = sb_ref[...]
    seg = seg_ref[...]

    def head_norm_rope(p, g):
        hi, lo = _split_bf16(p * p)
        ssum = (jnp.dot(hi, seg, preferred_element_type=F32)
                + jnp.dot(lo, seg, preferred_element_type=F32))
        n = p * lax.rsqrt(ssum * (1.0 / HEAD_DIM) + RMS_EPS) * g
        outs = []
        for c in range(GROUP_WIDTH // LANES):
            t = n[:, c * LANES:(c + 1) * LANES]
            up = pltpu.roll(t, LANES - ROPE_HALF, 1)
            dn = pltpu.roll(t, ROPE_HALF, 1)
            outs.append(t * cos_t + up * sa_t + dn * sb_t)
        return jnp.concatenate(outs, axis=1)

    scale = HEAD_DIM ** -0.5
    daq_ref[...] = (head_norm_rope(proj(0), qg_ref[...]) * scale).astype(BF16)
    dak_ref[...] = head_norm_rope(proj(1), kg_ref[...]).astype(BF16)
    dav_ref[...] = proj(2).astype(BF16)
    sbq_ref[...] = (proj(3) * scale).astype(BF16)
    sbk_ref[...] = proj(4).astype(BF16)
    sbv_ref[...] = proj(5).astype(BF16)


def _inproj(x2, g1, w_bf, qg, kg, cos_t, sa_t, sb_t, seg, seq):
    t = x2.shape[0]
    n_tiles = t // ROW_TILE
    seq_tiles = seq // ROW_TILE
    row = lambda i: (i, 0)
    const = lambda i: (0, 0)
    pos = lambda i: (i % seq_tiles, 0)
    out = jax.ShapeDtypeStruct((t, GROUP_WIDTH), BF16)
    return pl.pallas_call(
        _inproj_kernel,
        out_shape=[out] * 6,
        grid=(n_tiles,),
        in_specs=[
            pl.BlockSpec((ROW_TILE, D_MODEL), row),
            pl.BlockSpec((1, D_MODEL), const),
            pl.BlockSpec((D_MODEL, 6 * GROUP_WIDTH), const),
            pl.BlockSpec((1, GROUP_WIDTH), const),
            pl.BlockSpec((1, GROUP_WIDTH), const),
            pl.BlockSpec((ROW_TILE, LANES), pos),
            pl.BlockSpec((ROW_TILE, LANES), pos),
            pl.BlockSpec((ROW_TILE, LANES), pos),
            pl.BlockSpec((GROUP_WIDTH, GROUP_WIDTH), const),
        ],
        out_specs=[pl.BlockSpec((ROW_TILE, GROUP_WIDTH), row)] * 6,
        compiler_params=pltpu.CompilerParams(
            dimension_semantics=("parallel",), vmem_limit_bytes=VMEM_LIMIT),
        name="inproj",
    )(x2, g1, w_bf, qg, kg, cos_t, sa_t, sb_t, seg)


def _da_kernel(lam_ref, q_ref, k_ref, v_ref, g_ref, o_ref, m_ref, l_ref, acc_ref, *, out_scale):
    i = pl.program_id(2)
    blk = ATT_BLOCK
    nsub = DA_Q_ROWS // blk
    m_ref[...] = jnp.full_like(m_ref, -jnp.inf)
    l_ref[...] = jnp.zeros_like(l_ref)
    acc_ref[...] = jnp.zeros_like(acc_ref)

    def step(key0, span, first_chunk, masked):
        units = [(s, c) for s in range(2) for c in range(first_chunk, nsub)]
        if masked:
            row = lax.broadcasted_iota(jnp.int32, (blk, span), 0)
            col = lax.broadcasted_iota(jnp.int32, (blk, span), 1)
        kj = [k_ref[pl.ds(key0, span), s * HEAD_DIM:(s + 1) * HEAD_DIM] for s in range(2)]
        vj = v_ref[pl.ds(key0, span), :]
        sc = {}
        for s, c in units:
            qs = q_ref[c * blk:(c + 1) * blk, s * HEAD_DIM:(s + 1) * HEAD_DIM]
            sc[s, c] = lax.dot_general(qs, kj[s], (((1,), (1,)), ((), ())),
                                       preferred_element_type=F32)
            if masked:
                sc[s, c] = jnp.where(col <= row + (c - first_chunk) * blk, sc[s, c], -jnp.inf)
        wide = (blk, LANES)
        m_new = {}
        p = {}
        for u in units:
            parts = [sc[u][:, x:x + LANES] for x in range(0, span, LANES)]
            row_max = jnp.max(functools.reduce(jnp.maximum, parts), axis=-1, keepdims=True)
            m_new[u] = jnp.maximum(m_ref[u], jnp.broadcast_to(row_max, wide))
            p[u] = [jnp.exp(x - m_new[u]) for x in parts]
        pv = {u: jnp.dot(jnp.concatenate(p[u], axis=1).astype(BF16), vj,
                         preferred_element_type=F32) for u in units}
        for u in units:
            alpha = jnp.exp(m_ref[u] - m_new[u])
            m_ref[u] = m_new[u]
            l_ref[u] = alpha * l_ref[u] + functools.reduce(jnp.add, p[u])
            acc_ref[u] = alpha * acc_ref[u] + pv[u]

    def body(j, c):
        step(pl.multiple_of(j * DA_KEY_SPAN, DA_KEY_SPAN), DA_KEY_SPAN, 0, False)
        return c

    lax.fori_loop(0, i * (DA_Q_ROWS // DA_KEY_SPAN), body, 0)
    for d in range(nsub):
        step((i * nsub + d) * blk, blk, d, True)
    lam = lam_ref[0]
    outs = []
    for c in range(nsub):
        l0 = jnp.sum(l_ref[0, c], axis=-1, keepdims=True)
        l1 = jnp.sum(l_ref[1, c], axis=-1, keepdims=True)
        o = acc_ref[0, c] / l0 - lam * (acc_ref[1, c] / l1)
        ms = jnp.mean(o * o, axis=-1, keepdims=True)
        outs.append(o * lax.rsqrt(ms + RMS_EPS) * g_ref[...] * out_scale)
    o_ref[...] = jnp.concatenate(outs, axis=0).astype(BF16)


def _diff_attention(lam, q, k, v, g, batch, seq, out_scale):
    nq = seq // DA_Q_ROWS
    heads = GROUP_WIDTH // (2 * HEAD_DIM)
    return pl.pallas_call(
        functools.partial(_da_kernel, out_scale=out_scale),
        out_shape=jax.ShapeDtypeStruct(q.shape, BF16),
        grid=(batch, heads, nq),
        in_specs=[
            pl.BlockSpec(memory_space=pltpu.SMEM),
            pl.BlockSpec((DA_Q_ROWS, 2 * HEAD_DIM), lambda b, h, i: (b * nq + i, h)),
            pl.BlockSpec((seq, 2 * HEAD_DIM), lambda b, h, i: (b, h)),
            pl.BlockSpec((seq, 2 * HEAD_DIM), lambda b, h, i: (b, h)),
            pl.BlockSpec((1, 2 * HEAD_DIM), lambda b, h, i: (0, 0)),
        ],
        out_specs=pl.BlockSpec((DA_Q_ROWS, 2 * HEAD_DIM), lambda b, h, i: (b * nq + i, h)),
        scratch_shapes=[
            pltpu.VMEM((2, DA_Q_ROWS // ATT_BLOCK, ATT_BLOCK, LANES), F32),
            pltpu.VMEM((2, DA_Q_ROWS // ATT_BLOCK, ATT_BLOCK, LANES), F32),
            pltpu.VMEM((2, DA_Q_ROWS // ATT_BLOCK, ATT_BLOCK, 2 * HEAD_DIM), F32),
        ],
        compiler_params=pltpu.CompilerParams(
            dimension_semantics=("parallel", "parallel", "parallel"),
            vmem_limit_bytes=VMEM_LIMIT),
        name="diff_attention",
    )(lam, q, k, v, g)


def _sb_kernel(q_ref, k_ref, v_ref, tri2_ref, g_ref, o_ref, tail_ref, acc_ref):
    i = pl.program_id(2)
    blk = ATT_BLOCK
    nsub = SB_Q_ROWS // blk
    tri2 = tri2_ref[...]
    tail_ref[...] = jnp.zeros_like(tail_ref)
    acc_ref[...] = jnp.zeros_like(acc_ref)

    def step(j, first_chunk, masked):
        units = [(s, c) for s in range(2) for c in range(first_chunk, nsub)]
        if masked:
            row = lax.broadcasted_iota(jnp.int32, (blk, blk), 0)
            col = lax.broadcasted_iota(jnp.int32, (blk, blk), 1)
            offset = lambda c: col < row + (c - first_chunk) * blk
        kj = [k_ref[pl.ds(j * blk, blk), s * HEAD_DIM:(s + 1) * HEAD_DIM] for s in range(2)]
        vj = [v_ref[pl.ds(j * blk, blk), s * HEAD_DIM:(s + 1) * HEAD_DIM] for s in range(2)]
        z = {}
        for s, c in units:
            qs = q_ref[c * blk:(c + 1) * blk, s * HEAD_DIM:(s + 1) * HEAD_DIM]
            z[s, c] = lax.dot_general(qs, kj[s], (((1,), (1,)), ((), ())),
                                      preferred_element_type=F32)
        split = {}
        for u in units:
            sp = jnp.maximum(z[u], 0.0) + jnp.log(1.0 + jnp.exp(-jnp.abs(z[u])))
            if masked:
                sp = jnp.where(offset(u[1]), sp, 0.0)
            split[u] = jnp.concatenate(_split_bf16(sp), axis=1)
        suffix = {u: jnp.dot(split[u], tri2, preferred_element_type=F32) for u in units}
        a = {}
        for s, c in units:
            w = jnp.exp(z[s, c] - suffix[s, c] - tail_ref[s, c])
            if masked:
                w = jnp.where(offset(c), w, 0.0)
            a[s, c] = w.astype(BF16)
        for s, c in units:
            tail_ref[s, c] += suffix[s, c][:, 0:1]
            acc_ref[s, c] += jnp.dot(a[s, c], vj[s], preferred_element_type=F32)

    for d in reversed(range(nsub)):
        step(i * nsub + d, d, True)

    def body(t, c):
        for d in range(nsub):
            step((i - t) * nsub - 1 - d, 0, False)
        return c

    lax.fori_loop(0, i, body, 0)
    g = g_ref[...]
    outs = []
    for s in range(2):
        o = jnp.concatenate([acc_ref[s, c] for c in range(nsub)], axis=0)
        ms = jnp.mean(o * o, axis=-1, keepdims=True)
        outs.append(o * lax.rsqrt(ms + RMS_EPS) * g)
    o_ref[...] = jnp.concatenate(outs, axis=1).astype(BF16)


def _stick_breaking_attention(q, k, v, tri, g, batch, seq):
    nq = seq // SB_Q_ROWS
    pairs = GROUP_WIDTH // (2 * HEAD_DIM)
    return pl.pallas_call(
        _sb_kernel,
        out_shape=jax.ShapeDtypeStruct(q.shape, BF16),
        grid=(batch, pairs, nq),
        in_specs=[
            pl.BlockSpec((SB_Q_ROWS, 2 * HEAD_DIM), lambda b, h, i: (b * nq + i, h)),
            pl.BlockSpec((seq, 2 * HEAD_DIM), lambda b, h, i: (b, h)),
            pl.BlockSpec((seq, 2 * HEAD_DIM), lambda b, h, i: (b, h)),
            pl.BlockSpec((2 * ATT_BLOCK, ATT_BLOCK), lambda b, h, i: (0, 0)),
            pl.BlockSpec((1, HEAD_DIM), lambda b, h, i: (0, 0)),
        ],
        out_specs=pl.BlockSpec((SB_Q_ROWS, 2 * HEAD_DIM), lambda b, h, i: (b * nq + i, h)),
        scratch_shapes=[
            pltpu.VMEM((2, SB_Q_ROWS // ATT_BLOCK, ATT_BLOCK, 1), F32),
            pltpu.VMEM((2, SB_Q_ROWS // ATT_BLOCK, ATT_BLOCK, HEAD_DIM), F32),
        ],
        compiler_params=pltpu.CompilerParams(
            dimension_semantics=("parallel", "parallel", "parallel"),
            vmem_limit_bytes=VMEM_LIMIT),
        name="stick_breaking_attention",
    )(q, k, v, tri, g)


def _outproj_kernel(x_ref, oda_ref, osb_ref, w_ref, g2_ref, wrh_ref, wrl_ref, br_ref,
                    x1_ref, h2_ref, ids_ref, wts_ref):
    mix = (jnp.dot(oda_ref[...], w_ref[:GROUP_WIDTH, :], preferred_element_type=F32)
           + jnp.dot(osb_ref[...], w_ref[GROUP_WIDTH:, :], preferred_element_type=F32))
    x1 = x_ref[...] + mix
    x1_ref[...] = x1
    ms = jnp.mean(x1 * x1, axis=-1, keepdims=True)
    h2 = x1 * lax.rsqrt(ms + RMS_EPS) * g2_ref[...]
    h2_ref[...] = h2

    hi, lo = _split_bf16(h2)
    wrh = wrh_ref[...]
    logits = (jnp.dot(hi, wrh, preferred_element_type=F32)
              + jnp.dot(hi, wrl_ref[...], preferred_element_type=F32)
              + jnp.dot(lo, wrh, preferred_element_type=F32)) + br_ref[...]

    rows = logits.shape[0]
    lane = lax.broadcasted_iota(jnp.int32, (rows, LANES), 1).astype(F32)
    neg = -jnp.inf
    gl = jnp.where(lane < N_GROUPS, logits, neg)
    gmax = jnp.max(gl, axis=-1, keepdims=True)
    g_p = 1.0 / jnp.sum(jnp.exp(gl - gmax), axis=-1, keepdims=True)
    g_idx = jnp.min(jnp.where(gl == gmax, lane, float(LANES)), axis=-1, keepdims=True)
    first = N_GROUPS + g_idx * EXPERTS_PER_GROUP
    el = jnp.where((lane >= first) & (lane < first + EXPERTS_PER_GROUP), logits, neg)
    e1 = jnp.max(el, axis=-1, keepdims=True)
    i1 = jnp.min(jnp.where(el == e1, lane, float(LANES)), axis=-1, keepdims=True)
    el2 = jnp.where(lane == i1, neg, el)
    e2 = jnp.max(el2, axis=-1, keepdims=True)
    i2 = jnp.min(jnp.where(el2 == e2, lane, float(LANES)), axis=-1, keepdims=True)
    d = jnp.exp(e2 - e1)
    w1 = g_p / (1.0 + d)
    w2 = g_p * d / (1.0 + d)
    ids = jnp.where(lane == 0.0, i1 - N_GROUPS, jnp.where(lane == 1.0, i2 - N_GROUPS, 0.0))
    ids_ref[...] = ids.astype(jnp.int32)
    wts_ref[...] = jnp.where(lane == 0.0, w1, jnp.where(lane == 1.0, w2, 0.0))


def _outproj(x2, oda, osb, w_bf, g2, wr_hi, wr_lo, br):
    t = x2.shape[0]
    row = lambda i: (i, 0)
    const = lambda i: (0, 0)
    return pl.pallas_call(
        _outproj_kernel,
        out_shape=[
            jax.ShapeDtypeStruct((t, D_MODEL), F32),
            jax.ShapeDtypeStruct((t, D_MODEL), F32),
            jax.ShapeDtypeStruct((t, LANES), jnp.int32),
            jax.ShapeDtypeStruct((t, LANES), F32),
        ],
        grid=(t // ROW_TILE,),
        in_specs=[
            pl.BlockSpec((ROW_TILE, D_MODEL), row),
            pl.BlockSpec((ROW_TILE, GROUP_WIDTH), row),
            pl.BlockSpec((ROW_TILE, GROUP_WIDTH), row),
            pl.BlockSpec((D_MODEL, D_MODEL), const),
            pl.BlockSpec((1, D_MODEL), const),
            pl.BlockSpec((D_MODEL, LANES), const),
            pl.BlockSpec((D_MODEL, LANES), const),
            pl.BlockSpec((1, LANES), const),
        ],
        out_specs=[
            pl.BlockSpec((ROW_TILE, D_MODEL), row),
            pl.BlockSpec((ROW_TILE, D_MODEL), row),
            pl.BlockSpec((ROW_TILE, LANES), row),
            pl.BlockSpec((ROW_TILE, LANES), row),
        ],
        compiler_params=pltpu.CompilerParams(
            dimension_semantics=("parallel",), vmem_limit_bytes=VMEM_LIMIT),
        name="outproj_router",
    )(x2, oda, osb, w_bf, g2, wr_hi, wr_lo, br)


def _row_copy(src, src_row, dst, dst_row, sem):
    return pltpu.make_async_copy(src.at[pl.ds(src_row, 1)], dst.at[pl.ds(dst_row, 1)], sem)


def _dispatch_kernel(pos_ref, h_ref, zero_ref, xs_ref, sem):
    del zero_ref

    def issue(r, c):
        for s in range(2):
            _row_copy(h_ref, r, xs_ref, pos_ref[0, 0, 2 * r + s], sem).start(priority=s)
        return c

    lax.fori_loop(0, MOVE_ROWS, issue, 0, unroll=4)
    for s in range(2):
        pltpu.make_async_copy(h_ref, xs_ref.at[pl.ds(0, MOVE_ROWS)], sem).wait()


def _dispatch(pos3, h2, zeros):
    t = h2.shape[0]
    return pl.pallas_call(
        _dispatch_kernel,
        out_shape=jax.ShapeDtypeStruct(zeros.shape, F32),
        grid=(t // MOVE_ROWS,),
        in_specs=[
            pl.BlockSpec((1, 1, 2 * MOVE_ROWS), lambda i: (i, 0, 0), memory_space=pltpu.SMEM),
            pl.BlockSpec((MOVE_ROWS, D_MODEL), lambda i: (i, 0)),
            pl.BlockSpec(memory_space=pl.ANY),
        ],
        out_specs=pl.BlockSpec(memory_space=pl.ANY),
        scratch_shapes=[pltpu.SemaphoreType.DMA(())],
        input_output_aliases={2: 0},
        compiler_params=pltpu.CompilerParams(
            dimension_semantics=("arbitrary",), has_side_effects=True,
            vmem_limit_bytes=VMEM_LIMIT),
        name="dispatch",
    )(pos3, h2, zeros)


def _experts_kernel(te_ref, nv_ref, xs_ref, wg_ref, wu_ref, wd_ref, ys_ref):
    i = pl.program_id(0)

    @pl.when(i < nv_ref[0])
    def _():
        x = xs_ref[...].astype(BF16)
        hg = jnp.dot(x, wg_ref[0, 0].astype(BF16), preferred_element_type=F32)
        hu = jnp.dot(x, wu_ref[0, 0].astype(BF16), preferred_element_type=F32)
        act = (hg * jax.nn.sigmoid(hg) * hu).astype(BF16)
        ys_ref[...] = jnp.dot(act, wd_ref[0, 0].astype(BF16), preferred_element_type=F32)

    @pl.when(i >= nv_ref[0])
    def _():
        ys_ref[...] = jnp.zeros_like(ys_ref)


def _experts(tile_expert, n_valid, xs, w_gate, w_up, w_down, layer):
    p = xs.shape[0]
    grid_spec = pltpu.PrefetchScalarGridSpec(
        num_scalar_prefetch=2,
        grid=(p // EXPERT_TILE,),
        in_specs=[
            pl.BlockSpec((EXPERT_TILE, D_MODEL), lambda i, te, nv: (i, 0)),
            pl.BlockSpec((1, 1, D_MODEL, EXPERT_FF), lambda i, te, nv: (layer, te[i], 0, 0)),
            pl.BlockSpec((1, 1, D_MODEL, EXPERT_FF), lambda i, te, nv: (layer, te[i], 0, 0)),
            pl.BlockSpec((1, 1, EXPERT_FF, D_MODEL), lambda i, te, nv: (layer, te[i], 0, 0)),
        ],
        out_specs=pl.BlockSpec((EXPERT_TILE, D_MODEL), lambda i, te, nv: (i, 0)),
    )
    return pl.pallas_call(
        _experts_kernel,
        out_shape=jax.ShapeDtypeStruct((p, D_MODEL), F32),
        grid_spec=grid_spec,
        compiler_params=pltpu.CompilerParams(
            dimension_semantics=("arbitrary",), vmem_limit_bytes=VMEM_LIMIT),
        name="experts",
    )(tile_expert, n_valid, xs, w_gate, w_up, w_down)


def _combine_kernel(pos_ref, pos_next_ref, x1_ref, wts_ref, ys_ref, out_ref, buf, sem):
    i = pl.program_id(0)
    n = pl.num_programs(0)

    def gather(p_ref, slot):
        def issue(r, c):
            for s in range(2):
                _row_copy(ys_ref, p_ref[0, 0, 2 * r + s], buf.at[slot, s], r,
                          sem.at[slot]).start(priority=s)
            return c
        lax.fori_loop(0, MOVE_ROWS, issue, 0, unroll=4)

    @pl.when(i == 0)
    def _():
        gather(pos_ref, 0)

    @pl.when(i + 1 < n)
    def _():
        gather(pos_next_ref, (i + 1) % 2)

    slot = i % 2
    for s in range(2):
        pltpu.make_async_copy(ys_ref.at[pl.ds(0, MOVE_ROWS)], buf.at[slot, s], sem.at[slot]).wait()
    w = wts_ref[...]
    out_ref[...] = x1_ref[...] + w[:, 0:1] * buf[slot, 0] + w[:, 1:2] * buf[slot, 1]


def _combine(pos3, x1, wts, ys):
    t = x1.shape[0]
    steps = t // MOVE_ROWS
    return pl.pallas_call(
        _combine_kernel,
        out_shape=jax.ShapeDtypeStruct((t, D_MODEL), F32),
        grid=(steps,),
        in_specs=[
            pl.BlockSpec((1, 1, 2 * MOVE_ROWS), lambda i: (i, 0, 0), memory_space=pltpu.SMEM),
            pl.BlockSpec((1, 1, 2 * MOVE_ROWS), lambda i: (jnp.minimum(i + 1, steps - 1), 0, 0),
                         memory_space=pltpu.SMEM),
            pl.BlockSpec((MOVE_ROWS, D_MODEL), lambda i: (i, 0)),
            pl.BlockSpec((MOVE_ROWS, LANES), lambda i: (i, 0)),
            pl.BlockSpec(memory_space=pl.ANY),
        ],
        out_specs=pl.BlockSpec((MOVE_ROWS, D_MODEL), lambda i: (i, 0)),
        scratch_shapes=[pltpu.VMEM((2, 2, MOVE_ROWS, D_MODEL), F32),
                        pltpu.SemaphoreType.DMA((2,))],
        compiler_params=pltpu.CompilerParams(
            dimension_semantics=("arbitrary",), vmem_limit_bytes=VMEM_LIMIT),
        name="combine",
    )(pos3, pos3, x1, wts, ys)


def _routing_positions(ids, n_tiles):
    e = ids.reshape(-1)
    onehot = (e[:, None] == jnp.arange(N_EXPERTS, dtype=jnp.int32)[None, :]).astype(jnp.int32)
    csum = jnp.cumsum(onehot, axis=0)
    rank = jnp.sum(csum * onehot, axis=1) - 1
    counts = csum[-1]
    padded = ((counts + EXPERT_TILE - 1) // EXPERT_TILE) * EXPERT_TILE
    ends = jnp.cumsum(padded)
    starts = ends - padded
    pos = jnp.sum(starts[None, :] * onehot, axis=1) + rank
    tile_start = jnp.arange(n_tiles, dtype=jnp.int32) * EXPERT_TILE
    tile_expert = jnp.sum((tile_start[:, None] >= ends[None, :]).astype(jnp.int32), axis=1)
    tile_expert = jnp.minimum(tile_expert, N_EXPERTS - 1).astype(jnp.int32)
    n_valid = (ends[-1] // EXPERT_TILE).astype(jnp.int32).reshape(1)
    return pos.astype(jnp.int32), tile_expert, n_valid


def _rope_tables(seq):
    inv_freq = 1.0 / (ROPE_THETA ** (jnp.arange(0, 2 * ROPE_HALF, 2, dtype=F32) / (2 * ROPE_HALF)))
    ang = jnp.arange(seq, dtype=F32)[:, None] * inv_freq[None, :]
    cos, sin = jnp.cos(ang), jnp.sin(ang)
    ones = jnp.ones((seq, HEAD_DIM - 2 * ROPE_HALF), F32)
    zeros8 = jnp.zeros((seq, ROPE_HALF), F32)
    zeros = jnp.zeros((seq, HEAD_DIM - 2 * ROPE_HALF), F32)
    cos_h = jnp.concatenate([cos, cos, ones], axis=1)
    sa_h = jnp.concatenate([-sin, zeros8, zeros], axis=1)
    sb_h = jnp.concatenate([zeros8, sin, zeros], axis=1)
    rep = LANES // HEAD_DIM
    return jnp.tile(cos_h, (1, rep)), jnp.tile(sa_h, (1, rep)), jnp.tile(sb_h, (1, rep))


def kernel(x, norm1_g, w_in, q_norm_g, k_norm_g, lambda_q1, lambda_k1, lambda_q2, lambda_k2,
           da_out_g, sb_out_g, w_out, norm2_g, w_router_group, b_router_group,
           w_router_expert, b_router_expert, w_gate, w_up, w_down):
    batch, seq, d = x.shape
    depth = w_in.shape[0]
    t = batch * seq
    assert d == D_MODEL and seq % ROW_TILE == 0 and seq % ATT_BLOCK == 0 and t % MOVE_ROWS == 0

    cos_t, sa_t, sb_t = _rope_tables(seq)
    head_of = jnp.arange(GROUP_WIDTH) // HEAD_DIM
    seg = (head_of[:, None] == head_of[None, :]).astype(BF16)
    kk = jnp.arange(ATT_BLOCK)
    tri = (kk[:, None] >= kk[None, :]).astype(BF16)
    tri = jnp.concatenate([tri, tri], axis=0)
    n_rows = 2 * t + N_EXPERTS * EXPERT_TILE
    n_tiles = n_rows // EXPERT_TILE
    heads_per_group = GROUP_WIDTH // HEAD_DIM

    x2 = x.reshape(t, d)
    xs = jnp.zeros((n_rows, d), F32)
    for l in range(depth):
        lambda_init = 0.8 - 0.6 * math.exp(-0.3 * l)
        lam = (jnp.exp(jnp.sum(lambda_q1[l] * lambda_k1[l]))
               - jnp.exp(jnp.sum(lambda_q2[l] * lambda_k2[l])) + lambda_init).reshape(1).astype(F32)

        daq, dak, dav, sbq, sbk, sbv = _inproj(
            x2, norm1_g[l][None, :], w_in[l].astype(BF16),
            jnp.tile(q_norm_g[l], heads_per_group)[None, :],
            jnp.tile(k_norm_g[l], heads_per_group)[None, :],
            cos_t, sa_t, sb_t, seg, seq)

        o_da = _diff_attention(lam, daq, dak, dav, da_out_g[l][None, :], batch, seq,
                               1.0 - lambda_init)
        o_sb = _stick_breaking_attention(sbq, sbk, sbv, tri, sb_out_g[l][None, :], batch, seq)

        wr = jnp.concatenate([w_router_group[l], w_router_expert[l]], axis=1)
        wr = jnp.pad(wr, ((0, 0), (0, LANES - wr.shape[1])))
        wr_hi, wr_lo = _split_bf16(wr)
        br = jnp.concatenate([b_router_group[l], b_router_expert[l]])
        br = jnp.pad(br, (0, LANES - br.shape[0]))[None, :]
        x1, h2, ids, wts = _outproj(x2, o_da, o_sb, w_out[l].astype(BF16), norm2_g[l][None, :],
                                    wr_hi, wr_lo, br)

        pos, tile_expert, n_valid = _routing_positions(ids[:, :2], n_tiles)
        pos3 = pos.reshape(t // MOVE_ROWS, 1, 2 * MOVE_ROWS)
        xs = _dispatch(pos3, h2, xs)
        ys = _experts(tile_expert, n_valid, xs, w_gate, w_up, w_down, l)
        x2 = _combine(pos3, x1, wts, ys)
    return x2.reshape(batch, seq, d)
```

```python
import functools
import math

import jax
import jax.numpy as jnp
from jax import lax
from jax.experimental import pallas as pl
from jax.experimental.pallas import tpu as pltpu

F32 = jnp.float32
BF16 = jnp.bfloat16

D_MODEL = 1024
HEAD_DIM = 64
GROUP_WIDTH = 512
ROPE_HALF = 8
N_GROUPS = 4
EXPERTS_PER_GROUP = 8
N_EXPERTS = N_GROUPS * EXPERTS_PER_GROUP
EXPERT_FF = 256
RMS_EPS = 1e-6
ROPE_THETA = 500000.0
LANES = 128

ROW_TILE = 512
ATT_BLOCK = 256
SB_Q_ROWS = 1024
DA_Q_ROWS = 1024
DA_KEY_SPAN = 1024
EXPERT_TILE = 256
MOVE_ROWS = 256
DISPATCH_ROWS = 1024
VMEM_LIMIT = 48 * 1024 * 1024


def _split_bf16(x):
    hi = x.astype(BF16)
    lo = (x - hi.astype(F32)).astype(BF16)
    return hi, lo


def _inproj_kernel(x_ref, g1_ref, w_ref, qg_ref, kg_ref, cos_ref, sa_ref, sb_ref, seg_ref,
                   daq_ref, dak_ref, dav_ref, sbq_ref, sbk_ref, sbv_ref):
    x = x_ref[...]
    ms = jnp.mean(x * x, axis=-1, keepdims=True)
    h = (x * lax.rsqrt(ms + RMS_EPS) * g1_ref[...]).astype(BF16)

    def proj(c):
        return jnp.dot(h, w_ref[:, c * GROUP_WIDTH:(c + 1) * GROUP_WIDTH],
                       preferred_element_type=F32)

    cos_t = cos_ref[...]
    sa_t = sa_ref[...]
    sb_t = sb_ref[...]
    seg = seg_ref[...]

    def head_norm_rope(p, g):
        hi, lo = _split_bf16(p * p)
        ssum = (jnp.dot(hi, seg, preferred_element_type=F32)
                + jnp.dot(lo, seg, preferred_element_type=F32))
        n = p * lax.rsqrt(ssum * (1.0 / HEAD_DIM) + RMS_EPS) * g
        outs = []
        for c in range(GROUP_WIDTH // LANES):
            t = n[:, c * LANES:(c + 1) * LANES]
            up = pltpu.roll(t, LANES - ROPE_HALF, 1)
            dn = pltpu.roll(t, ROPE_HALF, 1)
            outs.append(t * cos_t + up * sa_t + dn * sb_t)
        return jnp.concatenate(outs, axis=1)

    scale = HEAD_DIM ** -0.5
    daq_ref[...] = (head_norm_rope(proj(0), qg_ref[...]) * scale).astype(BF16)
    dak_ref[...] = head_norm_rope(proj(1), kg_ref[...]).astype(BF16)
    dav_ref[...] = proj(2).astype(BF16)
    sbq_ref[...] = (proj(3) * scale).astype(BF16)
    sbk_ref[...] = proj(4).astype(BF16)
    sbv_ref[...] = proj(5).astype(BF16)


def _inproj(x2, g1, w_bf, qg, kg, cos_t, sa_t, sb_t, seg, seq):
    t = x2.shape[0]
    n_tiles = t // ROW_TILE
    seq_tiles = seq // ROW_TILE
    row = lambda i: (i, 0)
    const = lambda i: (0, 0)
    pos = lambda i: (i % seq_tiles, 0)
    out = jax.ShapeDtypeStruct((t, GROUP_WIDTH), BF16)
    return pl.pallas_call(
        _inproj_kernel,
        out_shape=[out] * 6,
        grid=(n_tiles,),
        in_specs=[
            pl.BlockSpec((ROW_TILE, D_MODEL), row),
            pl.BlockSpec((1, D_MODEL), const),
            pl.BlockSpec((D_MODEL, 6 * GROUP_WIDTH), const),
            pl.BlockSpec((1, GROUP_WIDTH), const),
            pl.BlockSpec((1, GROUP_WIDTH), const),
            pl.BlockSpec((ROW_TILE, LANES), pos),
            pl.BlockSpec((ROW_TILE, LANES), pos),
            pl.BlockSpec((ROW_TILE, LANES), pos),
            pl.BlockSpec((GROUP_WIDTH, GROUP_WIDTH), const),
        ],
        out_specs=[pl.BlockSpec((ROW_TILE, GROUP_WIDTH), row)] * 6,
        compiler_params=pltpu.CompilerParams(
            dimension_semantics=("parallel",), vmem_limit_bytes=VMEM_LIMIT),
        name="inproj",
    )(x2, g1, w_bf, qg, kg, cos_t, sa_t, sb_t, seg)


def _da_kernel(lam_ref, q_ref, k_ref, v_ref, g_ref, o_ref, m_ref, l_ref, acc_ref, *, out_scale):
    i = pl.program_id(2)
    blk = ATT_BLOCK
    nsub = DA_Q_ROWS // blk
    m_ref[...] = jnp.full_like(m_ref, -jnp.inf)
    l_ref[...] = jnp.zeros_like(l_ref)
    acc_ref[...] = jnp.zeros_like(acc_ref)

    def step(key0, span, first_chunk, masked):
        units = [(s, c) for s in range(2) for c in range(first_chunk, nsub)]
        if masked:
            row = lax.broadcasted_iota(jnp.int32, (blk, span), 0)
            col = lax.broadcasted_iota(jnp.int32, (blk, span), 1)
        kj = [k_ref[pl.ds(key0, span), s * HEAD_DIM:(s + 1) * HEAD_DIM] for s in range(2)]
        vj = v_ref[pl.ds(key0, span), :]
        sc = {}
        for s, c in units:
            qs = q_ref[c * blk:(c + 1) * blk, s * HEAD_DIM:(s + 1) * HEAD_DIM]
            sc[s, c] = lax.dot_general(qs, kj[s], (((1,), (1,)), ((), ())),
                                       preferred_element_type=F32)
            if masked:
                sc[s, c] = jnp.where(col <= row + (c - first_chunk) * blk, sc[s, c], -jnp.inf)
        wide = (blk, LANES)
        m_new = {}
        p = {}
        for u in units:
            parts = [sc[u][:, x:x + LANES] for x in range(0, span, LANES)]
            row_max = jnp.max(functools.reduce(jnp.maximum, parts), axis=-1, keepdims=True)
            m_new[u] = jnp.maximum(m_ref[u], jnp.broadcast_to(row_max, wide))
            p[u] = [jnp.exp(x - m_new[u]) for x in parts]
        pv = {u: jnp.dot(jnp.concatenate(p[u], axis=1).astype(BF16), vj,
                         preferred_element_type=F32) for u in units}
        for u in units:
            alpha = jnp.exp(m_ref[u] - m_new[u])
            m_ref[u] = m_new[u]
            l_ref[u] = alpha * l_ref[u] + functools.reduce(jnp.add, p[u])
            acc_ref[u] = alpha * acc_ref[u] + pv[u]

    def body(j, c):
        step(pl.multiple_of(j * DA_KEY_SPAN, DA_KEY_SPAN), DA_KEY_SPAN, 0, False)
        return c

    lax.fori_loop(0, i * (DA_Q_ROWS // DA_KEY_SPAN), body, 0)
    for d in range(nsub):
        step((i * nsub + d) * blk, blk, d, True)
    lam = lam_ref[0]
    outs = []
    for c in range(nsub):
        l0 = jnp.sum(l_ref[0, c], axis=-1, keepdims=True)
        l1 = jnp.sum(l_ref[1, c], axis=-1, keepdims=True)
        o = acc_ref[0, c] / l0 - lam * (acc_ref[1, c] / l1)
        ms = jnp.mean(o * o, axis=-1, keepdims=True)
        outs.append(o * lax.rsqrt(ms + RMS_EPS) * g_ref[...] * out_scale)
    o_ref[...] = jnp.concatenate(outs, axis=0).astype(BF16)


def _diff_attention(lam, q, k, v, g, batch, seq, out_scale):
    nq = seq // DA_Q_ROWS
    heads = GROUP_WIDTH // (2 * HEAD_DIM)
    return pl.pallas_call(
        functools.partial(_da_kernel, out_scale=out_scale),
        out_shape=jax.ShapeDtypeStruct(q.shape, BF16),
        grid=(batch, heads, nq),
        in_specs=[
            pl.BlockSpec(memory_space=pltpu.SMEM),
            pl.BlockSpec((DA_Q_ROWS, 2 * HEAD_DIM), lambda b, h, i: (b * nq + i, h)),
            pl.BlockSpec((seq, 2 * HEAD_DIM), lambda b, h, i: (b, h)),
            pl.BlockSpec((seq, 2 * HEAD_DIM), lambda b, h, i: (b, h)),
            pl.BlockSpec((1, 2 * HEAD_DIM), lambda b, h, i: (0, 0)),
        ],
        out_specs=pl.BlockSpec((DA_Q_ROWS, 2 * HEAD_DIM), lambda b, h, i: (b * nq + i, h)),
        scratch_shapes=[
            pltpu.VMEM((2, DA_Q_ROWS // ATT_BLOCK, ATT_BLOCK, LANES), F32),
            pltpu.VMEM((2, DA_Q_ROWS // ATT_BLOCK, ATT_BLOCK, LANES), F32),
            pltpu.VMEM((2, DA_Q_ROWS // ATT_BLOCK, ATT_BLOCK, 2 * HEAD_DIM), F32),
        ],
        compiler_params=pltpu.CompilerParams(
            dimension_semantics=("parallel", "parallel", "parallel"),
            vmem_limit_bytes=VMEM_LIMIT),
        name="diff_attention",
    )(lam, q, k, v, g)


def _sb_kernel(q_ref, k_ref, v_ref, tri2_ref, g_ref, o_ref, tail_ref, acc_ref):
    i = pl.program_id(2)
    blk = ATT_BLOCK
    nsub = SB_Q_ROWS // blk
    tri2 = tri2_ref[...]
    tail_ref[...] = jnp.zeros_like(tail_ref)
    acc_ref[...] = jnp.zeros_like(acc_ref)

    def step(j, first_chunk, masked):
        units = [(s, c) for s in range(2) for c in range(first_chunk, nsub)]
        if masked:
            row = lax.broadcasted_iota(jnp.int32, (blk, blk), 0)
            col = lax.broadcasted_iota(jnp.int32, (blk, blk), 1)
            offset = lambda c: col < row + (c - first_chunk) * blk
        kj = [k_ref[pl.ds(j * blk, blk), s * HEAD_DIM:(s + 1) * HEAD_DIM] for s in range(2)]
        vj = [v_ref[pl.ds(j * blk, blk), s * HEAD_DIM:(s + 1) * HEAD_DIM] for s in range(2)]
        z = {}
        for s, c in units:
            qs = q_ref[c * blk:(c + 1) * blk, s * HEAD_DIM:(s + 1) * HEAD_DIM]
            z[s, c] = lax.dot_general(qs, kj[s], (((1,), (1,)), ((), ())),
                                      preferred_element_type=F32)
        split = {}
        for u in units:
            sp = jnp.maximum(z[u], 0.0) + jnp.log(1.0 + jnp.exp(-jnp.abs(z[u])))
            if masked:
                sp = jnp.where(offset(u[1]), sp, 0.0)
            split[u] = jnp.concatenate(_split_bf16(sp), axis=1)
        suffix = {u: jnp.dot(split[u], tri2, preferred_element_type=F32) for u in units}
        a = {}
        for s, c in units:
            w = jnp.exp(z[s, c] - suffix[s, c] - tail_ref[s, c])
            if masked:
                w = jnp.where(offset(c), w, 0.0)
            a[s, c] = w.astype(BF16)
        for s, c in units:
            tail_ref[s, c] += suffix[s, c][:, 0:1]
            acc_ref[s, c] += jnp.dot(a[s, c], vj[s], preferred_element_type=F32)

    for d in reversed(range(nsub)):
        step(i * nsub + d, d, True)

    def body(t, c):
        for d in range(nsub):
            step((i - t) * nsub - 1 - d, 0, False)
        return c

    lax.fori_loop(0, i, body, 0)
    g = g_ref[...]
    outs = []
    for s in range(2):
        o = jnp.concatenate([acc_ref[s, c] for c in range(nsub)], axis=0)
        ms = jnp.mean(o * o, axis=-1, keepdims=True)
        outs.append(o * lax.rsqrt(ms + RMS_EPS) * g)
    o_ref[...] = jnp.concatenate(outs, axis=1).astype(BF16)


def _stick_breaking_attention(q, k, v, tri, g, batch, seq):
    nq = seq // SB_Q_ROWS
    pairs = GROUP_WIDTH // (2 * HEAD_DIM)
    return pl.pallas_call(
        _sb_kernel,
        out_shape=jax.ShapeDtypeStruct(q.shape, BF16),
        grid=(batch, pairs, nq),
        in_specs=[
            pl.BlockSpec((SB_Q_ROWS, 2 * HEAD_DIM), lambda b, h, i: (b * nq + i, h)),
            pl.BlockSpec((seq, 2 * HEAD_DIM), lambda b, h, i: (b, h)),
            pl.BlockSpec((seq, 2 * HEAD_DIM), lambda b, h, i: (b, h)),
            pl.BlockSpec((2 * ATT_BLOCK, ATT_BLOCK), lambda b, h, i: (0, 0)),
            pl.BlockSpec((1, HEAD_DIM), lambda b, h, i: (0, 0)),
        ],
        out_specs=pl.BlockSpec((SB_Q_ROWS, 2 * HEAD_DIM), lambda b, h, i: (b * nq + i, h)),
        scratch_shapes=[
            pltpu.VMEM((2, SB_Q_ROWS // ATT_BLOCK, ATT_BLOCK, 1), F32),
            pltpu.VMEM((2, SB_Q_ROWS // ATT_BLOCK, ATT_BLOCK, HEAD_DIM), F32),
        ],
        compiler_params=pltpu.CompilerParams(
            dimension_semantics=("parallel", "parallel", "parallel"),
            vmem_limit_bytes=VMEM_LIMIT),
        name="stick_breaking_attention",
    )(q, k, v, tri, g)


def _outproj_kernel(x_ref, oda_ref, osb_ref, w_ref, g2_ref, wrh_ref, wrl_ref, br_ref,
                    x1_ref, h2_ref, ids_ref, wts_ref):
    mix = (jnp.dot(oda_ref[...], w_ref[:GROUP_WIDTH, :], preferred_element_type=F32)
           + jnp.dot(osb_ref[...], w_ref[GROUP_WIDTH:, :], preferred_element_type=F32))
    x1 = x_ref[...] + mix
    x1_ref[...] = x1
    ms = jnp.mean(x1 * x1, axis=-1, keepdims=True)
    h2 = x1 * lax.rsqrt(ms + RMS_EPS) * g2_ref[...]
    h2_ref[...] = h2

    hi, lo = _split_bf16(h2)
    wrh = wrh_ref[...]
    logits = (jnp.dot(hi, wrh, preferred_element_type=F32)
              + jnp.dot(hi, wrl_ref[...], preferred_element_type=F32)
              + jnp.dot(lo, wrh, preferred_element_type=F32)) + br_ref[...]

    rows = logits.shape[0]
    lane = lax.broadcasted_iota(jnp.int32, (rows, LANES), 1).astype(F32)
    neg = -jnp.inf
    gl = jnp.where(lane < N_GROUPS, logits, neg)
    gmax = jnp.max(gl, axis=-1, keepdims=True)
    g_p = 1.0 / jnp.sum(jnp.exp(gl - gmax), axis=-1, keepdims=True)
    g_idx = jnp.min(jnp.where(gl == gmax, lane, float(LANES)), axis=-1, keepdims=True)
    first = N_GROUPS + g_idx * EXPERTS_PER_GROUP
    el = jnp.where((lane >= first) & (lane < first + EXPERTS_PER_GROUP), logits, neg)
    e1 = jnp.max(el, axis=-1, keepdims=True)
    i1 = jnp.min(jnp.where(el == e1, lane, float(LANES)), axis=-1, keepdims=True)
    el2 = jnp.where(lane == i1, neg, el)
    e2 = jnp.max(el2, axis=-1, keepdims=True)
    i2 = jnp.min(jnp.where(el2 == e2, lane, float(LANES)), axis=-1, keepdims=True)
    d = jnp.exp(e2 - e1)
    w1 = g_p / (1.0 + d)
    w2 = g_p * d / (1.0 + d)
    ids = jnp.where(lane == 0.0, i1 - N_GROUPS, jnp.where(lane == 1.0, i2 - N_GROUPS, 0.0))
    ids_ref[...] = ids.astype(jnp.int32)
    wts_ref[...] = jnp.where(lane == 0.0, w1, jnp.where(lane == 1.0, w2, 0.0))


def _outproj(x2, oda, osb, w_bf, g2, wr_hi, wr_lo, br):
    t = x2.shape[0]
    row = lambda i: (i, 0)
    const = lambda i: (0, 0)
    return pl.pallas_call(
        _outproj_kernel,
        out_shape=[
            jax.ShapeDtypeStruct((t, D_MODEL), F32),
            jax.ShapeDtypeStruct((t, D_MODEL), F32),
            jax.ShapeDtypeStruct((t, LANES), jnp.int32),
            jax.ShapeDtypeStruct((t, LANES), F32),
        ],
        grid=(t // ROW_TILE,),
        in_specs=[
            pl.BlockSpec((ROW_TILE, D_MODEL), row),
            pl.BlockSpec((ROW_TILE, GROUP_WIDTH), row),
            pl.BlockSpec((ROW_TILE, GROUP_WIDTH), row),
            pl.BlockSpec((D_MODEL, D_MODEL), const),
            pl.BlockSpec((1, D_MODEL), const),
            pl.BlockSpec((D_MODEL, LANES), const),
            pl.BlockSpec((D_MODEL, LANES), const),
            pl.BlockSpec((1, LANES), const),
        ],
        out_specs=[
            pl.BlockSpec((ROW_TILE, D_MODEL), row),
            pl.BlockSpec((ROW_TILE, D_MODEL), row),
            pl.BlockSpec((ROW_TILE, LANES), row),
            pl.BlockSpec((ROW_TILE, LANES), row),
        ],
        compiler_params=pltpu.CompilerParams(
            dimension_semantics=("parallel",), vmem_limit_bytes=VMEM_LIMIT),
        name="outproj_router",
    )(x2, oda, osb, w_bf, g2, wr_hi, wr_lo, br)


def _row_copy(src, src_row, dst, dst_row, sem):
    return pltpu.make_async_copy(src.at[pl.ds(src_row, 1)], dst.at[pl.ds(dst_row, 1)], sem)


def _dispatch_kernel(pos_ref, h_ref, zero_ref, xs_ref, sem):
    del zero_ref

    def issue(r, c):
        for s in range(2):
            _row_copy(h_ref, r, xs_ref, pos_ref[0, 0, 2 * r + s], sem).start(priority=s)
        return c

    lax.fori_loop(0, DISPATCH_ROWS, issue, 0, unroll=4)
    for s in range(2):
        pltpu.make_async_copy(h_ref, xs_ref.at[pl.ds(0, DISPATCH_ROWS)], sem).wait()


def _dispatch(pos, h2, zeros):
    t = h2.shape[0]
    return pl.pallas_call(
        _dispatch_kernel,
        out_shape=jax.ShapeDtypeStruct(zeros.shape, F32),
        grid=(t // DISPATCH_ROWS,),
        in_specs=[
            pl.BlockSpec((1, 1, 2 * DISPATCH_ROWS), lambda i: (i, 0, 0), memory_space=pltpu.SMEM),
            pl.BlockSpec((DISPATCH_ROWS, D_MODEL), lambda i: (i, 0)),
            pl.BlockSpec(memory_space=pl.ANY),
        ],
        out_specs=pl.BlockSpec(memory_space=pl.ANY),
        scratch_shapes=[pltpu.SemaphoreType.DMA(())],
        input_output_aliases={2: 0},
        compiler_params=pltpu.CompilerParams(
            dimension_semantics=("arbitrary",), has_side_effects=True,
            vmem_limit_bytes=VMEM_LIMIT),
        name="dispatch",
    )(pos.reshape(t // DISPATCH_ROWS, 1, 2 * DISPATCH_ROWS), h2, zeros)


def _experts_kernel(te_ref, nv_ref, xs_ref, wg_ref, wu_ref, wd_ref, ys_ref):
    i = pl.program_id(0)

    @pl.when(i < nv_ref[0])
    def _():
        x = xs_ref[...].astype(BF16)
        hg = jnp.dot(x, wg_ref[0, 0].astype(BF16), preferred_element_type=F32)
        hu = jnp.dot(x, wu_ref[0, 0].astype(BF16), preferred_element_type=F32)
        act = (hg * jax.nn.sigmoid(hg) * hu).astype(BF16)
        ys_ref[...] = jnp.dot(act, wd_ref[0, 0].astype(BF16), preferred_element_type=F32)

    @pl.when(i >= nv_ref[0])
    def _():
        ys_ref[...] = jnp.zeros_like(ys_ref)


def _experts(tile_expert, n_valid, xs, w_gate, w_up, w_down, layer):
    p = xs.shape[0]
    grid_spec = pltpu.PrefetchScalarGridSpec(
        num_scalar_prefetch=2,
        grid=(p // EXPERT_TILE,),
        in_specs=[
            pl.BlockSpec((EXPERT_TILE, D_MODEL), lambda i, te, nv: (i, 0)),
            pl.BlockSpec((1, 1, D_MODEL, EXPERT_FF), lambda i, te, nv: (layer, te[i], 0, 0)),
            pl.BlockSpec((1, 1, D_MODEL, EXPERT_FF), lambda i, te, nv: (layer, te[i], 0, 0)),
            pl.BlockSpec((1, 1, EXPERT_FF, D_MODEL), lambda i, te, nv: (layer, te[i], 0, 0)),
        ],
        out_specs=pl.BlockSpec((EXPERT_TILE, D_MODEL), lambda i, te, nv: (i, 0)),
    )
    return pl.pallas_call(
        _experts_kernel,
        out_shape=jax.ShapeDtypeStruct((p, D_MODEL), F32),
        grid_spec=grid_spec,
        compiler_params=pltpu.CompilerParams(
            dimension_semantics=("arbitrary",), vmem_limit_bytes=VMEM_LIMIT),
        name="experts",
    )(tile_expert, n_valid, xs, w_gate, w_up, w_down)


def _combine_kernel(pos_ref, pos_next_ref, x1_ref, wts_ref, ys_ref, out_ref, buf, sem):
    i = pl.program_id(0)
    n = pl.num_programs(0)

    def gather(p_ref, slot):
        def issue(r, c):
            for s in range(2):
                _row_copy(ys_ref, p_ref[0, 0, 2 * r + s], buf.at[slot, s], r,
                          sem.at[slot]).start(priority=s)
            return c
        lax.fori_loop(0, MOVE_ROWS, issue, 0, unroll=4)

    @pl.when(i == 0)
    def _():
        gather(pos_ref, 0)

    @pl.when(i + 1 < n)
    def _():
        gather(pos_next_ref, (i + 1) % 2)

    slot = i % 2
    for s in range(2):
        pltpu.make_async_copy(ys_ref.at[pl.ds(0, MOVE_ROWS)], buf.at[slot, s], sem.at[slot]).wait()
    w = wts_ref[...]
    out_ref[...] = x1_ref[...] + w[:, 0:1] * buf[slot, 0] + w[:, 1:2] * buf[slot, 1]


def _combine(pos3, x1, wts, ys):
    t = x1.shape[0]
    steps = t // MOVE_ROWS
    return pl.pallas_call(
        _combine_kernel,
        out_shape=jax.ShapeDtypeStruct((t, D_MODEL), F32),
        grid=(steps,),
        in_specs=[
            pl.BlockSpec((1, 1, 2 * MOVE_ROWS), lambda i: (i, 0, 0), memory_space=pltpu.SMEM),
            pl.BlockSpec((1, 1, 2 * MOVE_ROWS), lambda i: (jnp.minimum(i + 1, steps - 1), 0, 0),
                         memory_space=pltpu.SMEM),
            pl.BlockSpec((MOVE_ROWS, D_MODEL), lambda i: (i, 0)),
            pl.BlockSpec((MOVE_ROWS, LANES), lambda i: (i, 0)),
            pl.BlockSpec(memory_space=pl.ANY),
        ],
        out_specs=pl.BlockSpec((MOVE_ROWS, D_MODEL), lambda i: (i, 0)),
        scratch_shapes=[pltpu.VMEM((2, 2, MOVE_ROWS, D_MODEL), F32),
                        pltpu.SemaphoreType.DMA((2,))],
        compiler_params=pltpu.CompilerParams(
            dimension_semantics=("arbitrary",), vmem_limit_bytes=VMEM_LIMIT),
        name="combine",
    )(pos3, pos3, x1, wts, ys)


def _routing_positions(ids, n_tiles):
    e = ids.reshape(-1)
    onehot = (e[:, None] == jnp.arange(N_EXPERTS, dtype=jnp.int32)[None, :]).astype(jnp.int32)
    csum = jnp.cumsum(onehot, axis=0)
    rank = jnp.sum(csum * onehot, axis=1) - 1
    counts = csum[-1]
    padded = ((counts + EXPERT_TILE - 1) // EXPERT_TILE) * EXPERT_TILE
    ends = jnp.cumsum(padded)
    starts = ends - padded
    pos = jnp.sum(starts[None, :] * onehot, axis=1) + rank
    tile_start = jnp.arange(n_tiles, dtype=jnp.int32) * EXPERT_TILE
    tile_expert = jnp.sum((tile_start[:, None] >= ends[None, :]).astype(jnp.int32), axis=1)
    tile_expert = jnp.minimum(tile_expert, N_EXPERTS - 1).astype(jnp.int32)
    n_valid = (ends[-1] // EXPERT_TILE).astype(jnp.int32).reshape(1)
    return pos.astype(jnp.int32), tile_expert, n_valid


def _rope_tables(seq):
    inv_freq = 1.0 / (ROPE_THETA ** (jnp.arange(0, 2 * ROPE_HALF, 2, dtype=F32) / (2 * ROPE_HALF)))
    ang = jnp.arange(seq, dtype=F32)[:, None] * inv_freq[None, :]
    cos, sin = jnp.cos(ang), jnp.sin(ang)
    ones = jnp.ones((seq, HEAD_DIM - 2 * ROPE_HALF), F32)
    zeros8 = jnp.zeros((seq, ROPE_HALF), F32)
    zeros = jnp.zeros((seq, HEAD_DIM - 2 * ROPE_HALF), F32)
    cos_h = jnp.concatenate([cos, cos, ones], axis=1)
    sa_h = jnp.concatenate([-sin, zeros8, zeros], axis=1)
    sb_h = jnp.concatenate([zeros8, sin, zeros], axis=1)
    rep = LANES // HEAD_DIM
    return jnp.tile(cos_h, (1, rep)), jnp.tile(sa_h, (1, rep)), jnp.tile(sb_h, (1, rep))


def kernel(x, norm1_g, w_in, q_norm_g, k_norm_g, lambda_q1, lambda_k1, lambda_q2, lambda_k2,
           da_out_g, sb_out_g, w_out, norm2_g, w_router_group, b_router_group,
           w_router_expert, b_router_expert, w_gate, w_up, w_down):
    batch, seq, d = x.shape
    depth = w_in.shape[0]
    t = batch * seq
    assert d == D_MODEL and seq % ROW_TILE == 0 and seq % max(DA_Q_ROWS, SB_Q_ROWS) == 0
    assert t % MOVE_ROWS == 0 and t % DISPATCH_ROWS == 0

    cos_t, sa_t, sb_t = _rope_tables(seq)
    head_of = jnp.arange(GROUP_WIDTH) // HEAD_DIM
    seg = (head_of[:, None] == head_of[None, :]).astype(BF16)
    kk = jnp.arange(ATT_BLOCK)
    tri = (kk[:, None] >= kk[None, :]).astype(BF16)
    tri = jnp.concatenate([tri, tri], axis=0)
    n_rows = 2 * t + N_EXPERTS * EXPERT_TILE
    n_tiles = n_rows // EXPERT_TILE
    heads_per_group = GROUP_WIDTH // HEAD_DIM

    x2 = x.reshape(t, d)
    xs = jnp.zeros((n_rows, d), F32)
    for l in range(depth):
        lambda_init = 0.8 - 0.6 * math.exp(-0.3 * l)
        lam = (jnp.exp(jnp.sum(lambda_q1[l] * lambda_k1[l]))
               - jnp.exp(jnp.sum(lambda_q2[l] * lambda_k2[l])) + lambda_init).reshape(1).astype(F32)

        daq, dak, dav, sbq, sbk, sbv = _inproj(
            x2, norm1_g[l][None, :], w_in[l].astype(BF16),
            jnp.tile(q_norm_g[l], heads_per_group)[None, :],
            jnp.tile(k_norm_g[l], heads_per_group)[None, :],
            cos_t, sa_t, sb_t, seg, seq)

        o_da = _diff_attention(lam, daq, dak, dav, da_out_g[l][None, :], batch, seq,
                               1.0 - lambda_init)
        o_sb = _stick_breaking_attention(sbq, sbk, sbv, tri, sb_out_g[l][None, :], batch, seq)

        wr = jnp.concatenate([w_router_group[l], w_router_expert[l]], axis=1)
        wr = jnp.pad(wr, ((0, 0), (0, LANES - wr.shape[1])))
        wr_hi, wr_lo = _split_bf16(wr)
        br = jnp.concatenate([b_router_group[l], b_router_expert[l]])
        br = jnp.pad(br, (0, LANES - br.shape[0]))[None, :]
        x1, h2, ids, wts = _outproj(x2, o_da, o_sb, w_out[l].astype(BF16), norm2_g[l][None, :],
                                    wr_hi, wr_lo, br)

        pos, tile_expert, n_valid = _routing_positions(ids[:, :2], n_tiles)
        pos3 = pos.reshape(t // MOVE_ROWS, 1, 2 * MOVE_ROWS)
        xs = _dispatch(pos, h2, xs)
        ys = _experts(tile_expert, n_valid, xs, w_gate, w_up, w_down, l)
        x2 = _combine(pos3, x1, wts, ys)
    return x2.reshape(batch, seq, d)
```

```python
import functools
import math

import jax
import jax.numpy as jnp
from jax import lax
from jax.experimental import pallas as pl
from jax.experimental.pallas import tpu as pltpu

F32 = jnp.float32
BF16 = jnp.bfloat16

D_MODEL = 1024
HEAD_DIM = 64
GROUP_WIDTH = 512
ROPE_HALF = 8
N_GROUPS = 4
EXPERTS_PER_GROUP = 8
N_EXPERTS = N_GROUPS * EXPERTS_PER_GROUP
EXPERT_FF = 256
RMS_EPS = 1e-6
ROPE_THETA = 500000.0
LANES = 128

ROW_TILE = 512
ATT_BLOCK = 256
SB_Q_ROWS = 1024
DA_Q_ROWS = 1024
DA_KEY_SPAN = 1024
EXPERT_TILE = 512
MOVE_ROWS = 256
DISPATCH_ROWS = 1024
VMEM_LIMIT = 48 * 1024 * 1024


def _split_bf16(x):
    hi = x.astype(BF16)
    lo = (x - hi.astype(F32)).astype(BF16)
    return hi, lo


def _inproj_kernel(x_ref, g1_ref, w_ref, qg_ref, kg_ref, cos_ref, sa_ref, sb_ref, seg_ref,
                   daq_ref, dak_ref, dav_ref, sbq_ref, sbk_ref, sbv_ref):
    x = x_ref[...]
    ms = jnp.mean(x * x, axis=-1, keepdims=True)
    h = (x * lax.rsqrt(ms + RMS_EPS) * g1_ref[...]).astype(BF16)

    def proj(c):
        return jnp.dot(h, w_ref[:, c * GROUP_WIDTH:(c + 1) * GROUP_WIDTH],
                       preferred_element_type=F32)

    cos_t = cos_ref[...]
    sa_t = sa_ref[...]
    sb_t = sb_ref[...]
    seg = seg_ref[...]

    def head_norm_rope(p, g):
        hi, lo = _split_bf16(p * p)
        ssum = (jnp.dot(hi, seg, preferred_element_type=F32)
                + jnp.dot(lo, seg, preferred_element_type=F32))
        n = p * lax.rsqrt(ssum * (1.0 / HEAD_DIM) + RMS_EPS) * g
        outs = []
        for c in range(GROUP_WIDTH // LANES):
            t = n[:, c * LANES:(c + 1) * LANES]
            up = pltpu.roll(t, LANES - ROPE_HALF, 1)
            dn = pltpu.roll(t, ROPE_HALF, 1)
            outs.append(t * cos_t + up * sa_t + dn * sb_t)
        return jnp.concatenate(outs, axis=1)

    scale = HEAD_DIM ** -0.5
    daq_ref[...] = (head_norm_rope(proj(0), qg_ref[...]) * scale).astype(BF16)
    dak_ref[...] = head_norm_rope(proj(1), kg_ref[...]).astype(BF16)
    dav_ref[...] = proj(2).astype(BF16)
    sbq_ref[...] = (proj(3) * scale).astype(BF16)
    sbk_ref[...] = proj(4).astype(BF16)
    sbv_ref[...] = proj(5).astype(BF16)


def _inproj(x2, g1, w_bf, qg, kg, cos_t, sa_t, sb_t, seg, seq):
    t = x2.shape[0]
    n_tiles = t // ROW_TILE
    seq_tiles = seq // ROW_TILE
    row = lambda i: (i, 0)
    const = lambda i: (0, 0)
    pos = lambda i: (i % seq_tiles, 0)
    out = jax.ShapeDtypeStruct((t, GROUP_WIDTH), BF16)
    return pl.pallas_call(
        _inproj_kernel,
        out_shape=[out] * 6,
        grid=(n_tiles,),
        in_specs=[
            pl.BlockSpec((ROW_TILE, D_MODEL), row),
            pl.BlockSpec((1, D_MODEL), const),
            pl.BlockSpec((D_MODEL, 6 * GROUP_WIDTH), const),
            pl.BlockSpec((1, GROUP_WIDTH), const),
            pl.BlockSpec((1, GROUP_WIDTH), const),
            pl.BlockSpec((ROW_TILE, LANES), pos),
            pl.BlockSpec((ROW_TILE, LANES), pos),
            pl.BlockSpec((ROW_TILE, LANES), pos),
            pl.BlockSpec((GROUP_WIDTH, GROUP_WIDTH), const),
        ],
        out_specs=[pl.BlockSpec((ROW_TILE, GROUP_WIDTH), row)] * 6,
        compiler_params=pltpu.CompilerParams(
            dimension_semantics=("parallel",), vmem_limit_bytes=VMEM_LIMIT),
        name="inproj",
    )(x2, g1, w_bf, qg, kg, cos_t, sa_t, sb_t, seg)


def _da_kernel(lam_ref, q_ref, k_ref, v_ref, g_ref, o_ref, m_ref, l_ref, acc_ref, *, out_scale):
    i = pl.program_id(2)
    blk = ATT_BLOCK
    nsub = DA_Q_ROWS // blk
    m_ref[...] = jnp.full_like(m_ref, -jnp.inf)
    l_ref[...] = jnp.zeros_like(l_ref)
    acc_ref[...] = jnp.zeros_like(acc_ref)

    def step(key0, span, first_chunk, masked):
        units = [(s, c) for s in range(2) for c in range(first_chunk, nsub)]
        if masked:
            row = lax.broadcasted_iota(jnp.int32, (blk, span), 0)
            col = lax.broadcasted_iota(jnp.int32, (blk, span), 1)
        kj = [k_ref[pl.ds(key0, span), s * HEAD_DIM:(s + 1) * HEAD_DIM] for s in range(2)]
        vj = v_ref[pl.ds(key0, span), :]
        sc = {}
        for s, c in units:
            qs = q_ref[c * blk:(c + 1) * blk, s * HEAD_DIM:(s + 1) * HEAD_DIM]
            sc[s, c] = lax.dot_general(qs, kj[s], (((1,), (1,)), ((), ())),
                                       preferred_element_type=F32)
            if masked:
                sc[s, c] = jnp.where(col <= row + (c - first_chunk) * blk, sc[s, c], -jnp.inf)
        wide = (blk, LANES)
        m_new = {}
        p = {}
        for u in units:
            parts = [sc[u][:, x:x + LANES] for x in range(0, span, LANES)]
            row_max = jnp.max(functools.reduce(jnp.maximum, parts), axis=-1, keepdims=True)
            m_new[u] = jnp.maximum(m_ref[u], jnp.broadcast_to(row_max, wide))
            p[u] = [jnp.exp(x - m_new[u]) for x in parts]
        pv = {u: jnp.dot(jnp.concatenate(p[u], axis=1).astype(BF16), vj,
                         preferred_element_type=F32) for u in units}
        for u in units:
            alpha = jnp.exp(m_ref[u] - m_new[u])
            m_ref[u] = m_new[u]
            l_ref[u] = alpha * l_ref[u] + functools.reduce(jnp.add, p[u])
            acc_ref[u] = alpha * acc_ref[u] + pv[u]

    def body(j, c):
        step(pl.multiple_of(j * DA_KEY_SPAN, DA_KEY_SPAN), DA_KEY_SPAN, 0, False)
        return c

    lax.fori_loop(0, i * (DA_Q_ROWS // DA_KEY_SPAN), body, 0)
    for d in range(nsub):
        step((i * nsub + d) * blk, blk, d, True)
    lam = lam_ref[0]
    outs = []
    for c in range(nsub):
        l0 = jnp.sum(l_ref[0, c], axis=-1, keepdims=True)
        l1 = jnp.sum(l_ref[1, c], axis=-1, keepdims=True)
        o = acc_ref[0, c] / l0 - lam * (acc_ref[1, c] / l1)
        ms = jnp.mean(o * o, axis=-1, keepdims=True)
        outs.append(o * lax.rsqrt(ms + RMS_EPS) * g_ref[...] * out_scale)
    o_ref[...] = jnp.concatenate(outs, axis=0).astype(BF16)


def _diff_attention(lam, q, k, v, g, batch, seq, out_scale):
    nq = seq // DA_Q_ROWS
    heads = GROUP_WIDTH // (2 * HEAD_DIM)
    return pl.pallas_call(
        functools.partial(_da_kernel, out_scale=out_scale),
        out_shape=jax.ShapeDtypeStruct(q.shape, BF16),
        grid=(batch, heads, nq),
        in_specs=[
            pl.BlockSpec(memory_space=pltpu.SMEM),
            pl.BlockSpec((DA_Q_ROWS, 2 * HEAD_DIM), lambda b, h, i: (b * nq + i, h)),
            pl.BlockSpec((seq, 2 * HEAD_DIM), lambda b, h, i: (b, h)),
            pl.BlockSpec((seq, 2 * HEAD_DIM), lambda b, h, i: (b, h)),
            pl.BlockSpec((1, 2 * HEAD_DIM), lambda b, h, i: (0, 0)),
        ],
        out_specs=pl.BlockSpec((DA_Q_ROWS, 2 * HEAD_DIM), lambda b, h, i: (b * nq + i, h)),
        scratch_shapes=[
            pltpu.VMEM((2, DA_Q_ROWS // ATT_BLOCK, ATT_BLOCK, LANES), F32),
            pltpu.VMEM((2, DA_Q_ROWS // ATT_BLOCK, ATT_BLOCK, LANES), F32),
            pltpu.VMEM((2, DA_Q_ROWS // ATT_BLOCK, ATT_BLOCK, 2 * HEAD_DIM), F32),
        ],
        compiler_params=pltpu.CompilerParams(
            dimension_semantics=("parallel", "parallel", "parallel"),
            vmem_limit_bytes=VMEM_LIMIT),
        name="diff_attention",
    )(lam, q, k, v, g)


def _sb_kernel(q_ref, k_ref, v_ref, tri2_ref, g_ref, o_ref, tail_ref, acc_ref):
    i = pl.program_id(2)
    blk = ATT_BLOCK
    nsub = SB_Q_ROWS // blk
    tri2 = tri2_ref[...]
    tail_ref[...] = jnp.zeros_like(tail_ref)
    acc_ref[...] = jnp.zeros_like(acc_ref)

    def step(j, first_chunk, masked):
        units = [(s, c) for s in range(2) for c in range(first_chunk, nsub)]
        if masked:
            row = lax.broadcasted_iota(jnp.int32, (blk, blk), 0)
            col = lax.broadcasted_iota(jnp.int32, (blk, blk), 1)
            offset = lambda c: col < row + (c - first_chunk) * blk
        kj = [k_ref[pl.ds(j * blk, blk), s * HEAD_DIM:(s + 1) * HEAD_DIM] for s in range(2)]
        vj = [v_ref[pl.ds(j * blk, blk), s * HEAD_DIM:(s + 1) * HEAD_DIM] for s in range(2)]
        z = {}
        for s, c in units:
            qs = q_ref[c * blk:(c + 1) * blk, s * HEAD_DIM:(s + 1) * HEAD_DIM]
            z[s, c] = lax.dot_general(qs, kj[s], (((1,), (1,)), ((), ())),
                                      preferred_element_type=F32)
        split = {}
        for u in units:
            sp = jnp.maximum(z[u], 0.0) + jnp.log(1.0 + jnp.exp(-jnp.abs(z[u])))
            if masked:
                sp = jnp.where(offset(u[1]), sp, 0.0)
            split[u] = jnp.concatenate(_split_bf16(sp), axis=1)
        suffix = {u: jnp.dot(split[u], tri2, preferred_element_type=F32) for u in units}
        a = {}
        for s, c in units:
            w = jnp.exp(z[s, c] - suffix[s, c] - tail_ref[s, c])
            if masked:
                w = jnp.where(offset(c), w, 0.0)
            a[s, c] = w.astype(BF16)
        for s, c in units:
            tail_ref[s, c] += suffix[s, c][:, 0:1]
            acc_ref[s, c] += jnp.dot(a[s, c], vj[s], preferred_element_type=F32)

    for d in reversed(range(nsub)):
        step(i * nsub + d, d, True)

    def body(t, c):
        for d in range(nsub):
            step((i - t) * nsub - 1 - d, 0, False)
        return c

    lax.fori_loop(0, i, body, 0)
    g = g_ref[...]
    outs = []
    for s in range(2):
        o = jnp.concatenate([acc_ref[s, c] for c in range(nsub)], axis=0)
        ms = jnp.mean(o * o, axis=-1, keepdims=True)
        outs.append(o * lax.rsqrt(ms + RMS_EPS) * g)
    o_ref[...] = jnp.concatenate(outs, axis=1).astype(BF16)


def _stick_breaking_attention(q, k, v, tri, g, batch, seq):
    nq = seq // SB_Q_ROWS
    pairs = GROUP_WIDTH // (2 * HEAD_DIM)
    return pl.pallas_call(
        _sb_kernel,
        out_shape=jax.ShapeDtypeStruct(q.shape, BF16),
        grid=(batch, pairs, nq),
        in_specs=[
            pl.BlockSpec((SB_Q_ROWS, 2 * HEAD_DIM), lambda b, h, i: (b * nq + i, h)),
            pl.BlockSpec((seq, 2 * HEAD_DIM), lambda b, h, i: (b, h)),
            pl.BlockSpec((seq, 2 * HEAD_DIM), lambda b, h, i: (b, h)),
            pl.BlockSpec((2 * ATT_BLOCK, ATT_BLOCK), lambda b, h, i: (0, 0)),
            pl.BlockSpec((1, HEAD_DIM), lambda b, h, i: (0, 0)),
        ],
        out_specs=pl.BlockSpec((SB_Q_ROWS, 2 * HEAD_DIM), lambda b, h, i: (b * nq + i, h)),
        scratch_shapes=[
            pltpu.VMEM((2, SB_Q_ROWS // ATT_BLOCK, ATT_BLOCK, 1), F32),
            pltpu.VMEM((2, SB_Q_ROWS // ATT_BLOCK, ATT_BLOCK, HEAD_DIM), F32),
        ],
        compiler_params=pltpu.CompilerParams(
            dimension_semantics=("parallel", "parallel", "parallel"),
            vmem_limit_bytes=VMEM_LIMIT),
        name="stick_breaking_attention",
    )(q, k, v, tri, g)


def _outproj_kernel(x_ref, oda_ref, osb_ref, w_ref, g2_ref, wrh_ref, wrl_ref, br_ref,
                    x1_ref, h2_ref, ids_ref, wts_ref):
    mix = (jnp.dot(oda_ref[...], w_ref[:GROUP_WIDTH, :], preferred_element_type=F32)
           + jnp.dot(osb_ref[...], w_ref[GROUP_WIDTH:, :], preferred_element_type=F32))
    x1 = x_ref[...] + mix
    x1_ref[...] = x1
    ms = jnp.mean(x1 * x1, axis=-1, keepdims=True)
    h2 = x1 * lax.rsqrt(ms + RMS_EPS) * g2_ref[...]
    h2_ref[...] = h2

    hi, lo = _split_bf16(h2)
    wrh = wrh_ref[...]
    logits = (jnp.dot(hi, wrh, preferred_element_type=F32)
              + jnp.dot(hi, wrl_ref[...], preferred_element_type=F32)
              + jnp.dot(lo, wrh, preferred_element_type=F32)) + br_ref[...]

    rows = logits.shape[0]
    lane = lax.broadcasted_iota(jnp.int32, (rows, LANES), 1).astype(F32)
    neg = -jnp.inf
    gl = jnp.where(lane < N_GROUPS, logits, neg)
    gmax = jnp.max(gl, axis=-1, keepdims=True)
    g_p = 1.0 / jnp.sum(jnp.exp(gl - gmax), axis=-1, keepdims=True)
    g_idx = jnp.min(jnp.where(gl == gmax, lane, float(LANES)), axis=-1, keepdims=True)
    first = N_GROUPS + g_idx * EXPERTS_PER_GROUP
    el = jnp.where((lane >= first) & (lane < first + EXPERTS_PER_GROUP), logits, neg)
    e1 = jnp.max(el, axis=-1, keepdims=True)
    i1 = jnp.min(jnp.where(el == e1, lane, float(LANES)), axis=-1, keepdims=True)
    el2 = jnp.where(lane == i1, neg, el)
    e2 = jnp.max(el2, axis=-1, keepdims=True)
    i2 = jnp.min(jnp.where(el2 == e2, lane, float(LANES)), axis=-1, keepdims=True)
    d = jnp.exp(e2 - e1)
    w1 = g_p / (1.0 + d)
    w2 = g_p * d / (1.0 + d)
    ids = jnp.where(lane == 0.0, i1 - N_GROUPS, jnp.where(lane == 1.0, i2 - N_GROUPS, 0.0))
    ids_ref[...] = ids.astype(jnp.int32)
    wts_ref[...] = jnp.where(lane == 0.0, w1, jnp.where(lane == 1.0, w2, 0.0))


def _outproj(x2, oda, osb, w_bf, g2, wr_hi, wr_lo, br):
    t = x2.shape[0]
    row = lambda i: (i, 0)
    const = lambda i: (0, 0)
    return pl.pallas_call(
        _outproj_kernel,
        out_shape=[
            jax.ShapeDtypeStruct((t, D_MODEL), F32),
            jax.ShapeDtypeStruct((t, D_MODEL), F32),
            jax.ShapeDtypeStruct((t, LANES), jnp.int32),
            jax.ShapeDtypeStruct((t, LANES), F32),
        ],
        grid=(t // ROW_TILE,),
        in_specs=[
            pl.BlockSpec((ROW_TILE, D_MODEL), row),
            pl.BlockSpec((ROW_TILE, GROUP_WIDTH), row),
            pl.BlockSpec((ROW_TILE, GROUP_WIDTH), row),
            pl.BlockSpec((D_MODEL, D_MODEL), const),
            pl.BlockSpec((1, D_MODEL), const),
            pl.BlockSpec((D_MODEL, LANES), const),
            pl.BlockSpec((D_MODEL, LANES), const),
            pl.BlockSpec((1, LANES), const),
        ],
        out_specs=[
            pl.BlockSpec((ROW_TILE, D_MODEL), row),
            pl.BlockSpec((ROW_TILE, D_MODEL), row),
            pl.BlockSpec((ROW_TILE, LANES), row),
            pl.BlockSpec((ROW_TILE, LANES), row),
        ],
        compiler_params=pltpu.CompilerParams(
            dimension_semantics=("parallel",), vmem_limit_bytes=VMEM_LIMIT),
        name="outproj_router",
    )(x2, oda, osb, w_bf, g2, wr_hi, wr_lo, br)


def _row_copy(src, src_row, dst, dst_row, sem):
    return pltpu.make_async_copy(src.at[pl.ds(src_row, 1)], dst.at[pl.ds(dst_row, 1)], sem)


def _dispatch_kernel(pos_ref, h_ref, zero_ref, xs_ref, sem):
    del zero_ref

    def issue(r, c):
        for s in range(2):
            _row_copy(h_ref, r, xs_ref, pos_ref[0, 0, 2 * r + s], sem).start(priority=s)
        return c

    lax.fori_loop(0, DISPATCH_ROWS, issue, 0, unroll=4)
    for s in range(2):
        pltpu.make_async_copy(h_ref, xs_ref.at[pl.ds(0, DISPATCH_ROWS)], sem).wait()


def _dispatch(pos, h2, zeros):
    t = h2.shape[0]
    return pl.pallas_call(
        _dispatch_kernel,
        out_shape=jax.ShapeDtypeStruct(zeros.shape, F32),
        grid=(t // DISPATCH_ROWS,),
        in_specs=[
            pl.BlockSpec((1, 1, 2 * DISPATCH_ROWS), lambda i: (i, 0, 0), memory_space=pltpu.SMEM),
            pl.BlockSpec((DISPATCH_ROWS, D_MODEL), lambda i: (i, 0)),
            pl.BlockSpec(memory_space=pl.ANY),
        ],
        out_specs=pl.BlockSpec(memory_space=pl.ANY),
        scratch_shapes=[pltpu.SemaphoreType.DMA(())],
        input_output_aliases={2: 0},
        compiler_params=pltpu.CompilerParams(
            dimension_semantics=("arbitrary",), has_side_effects=True,
            vmem_limit_bytes=VMEM_LIMIT),
        name="dispatch",
    )(pos.reshape(t // DISPATCH_ROWS, 1, 2 * DISPATCH_ROWS), h2, zeros)


def _experts_kernel(te_ref, nv_ref, xs_ref, wg_ref, wu_ref, wd_ref, ys_ref):
    i = pl.program_id(0)

    @pl.when(i < nv_ref[0])
    def _():
        x = xs_ref[...].astype(BF16)
        hg = jnp.dot(x, wg_ref[0, 0].astype(BF16), preferred_element_type=F32)
        hu = jnp.dot(x, wu_ref[0, 0].astype(BF16), preferred_element_type=F32)
        act = (hg * jax.nn.sigmoid(hg) * hu).astype(BF16)
        ys_ref[...] = jnp.dot(act, wd_ref[0, 0].astype(BF16), preferred_element_type=F32)

    @pl.when(i >= nv_ref[0])
    def _():
        ys_ref[...] = jnp.zeros_like(ys_ref)


def _experts(tile_expert, n_valid, xs, w_gate, w_up, w_down, layer):
    p = xs.shape[0]
    grid_spec = pltpu.PrefetchScalarGridSpec(
        num_scalar_prefetch=2,
        grid=(p // EXPERT_TILE,),
        in_specs=[
            pl.BlockSpec((EXPERT_TILE, D_MODEL), lambda i, te, nv: (i, 0)),
            pl.BlockSpec((1, 1, D_MODEL, EXPERT_FF), lambda i, te, nv: (layer, te[i], 0, 0)),
            pl.BlockSpec((1, 1, D_MODEL, EXPERT_FF), lambda i, te, nv: (layer, te[i], 0, 0)),
            pl.BlockSpec((1, 1, EXPERT_FF, D_MODEL), lambda i, te, nv: (layer, te[i], 0, 0)),
        ],
        out_specs=pl.BlockSpec((EXPERT_TILE, D_MODEL), lambda i, te, nv: (i, 0)),
    )
    return pl.pallas_call(
        _experts_kernel,
        out_shape=jax.ShapeDtypeStruct((p, D_MODEL), F32),
        grid_spec=grid_spec,
        compiler_params=pltpu.CompilerParams(
            dimension_semantics=("arbitrary",), vmem_limit_bytes=VMEM_LIMIT),
        name="experts",
    )(tile_expert, n_valid, xs, w_gate, w_up, w_down)


def _combine_kernel(pos_ref, pos_next_ref, x1_ref, wts_ref, ys_ref, out_ref, buf, sem):
    i = pl.program_id(0)
    n = pl.num_programs(0)

    def gather(p_ref, slot):
        def issue(r, c):
            for s in range(2):
                _row_copy(ys_ref, p_ref[0, 0, 2 * r + s], buf.at[slot, s], r,
                          sem.at[slot]).start(priority=s)
            return c
        lax.fori_loop(0, MOVE_ROWS, issue, 0, unroll=4)

    @pl.when(i == 0)
    def _():
        gather(pos_ref, 0)

    @pl.when(i + 1 < n)
    def _():
        gather(pos_next_ref, (i + 1) % 2)

    slot = i % 2
    for s in range(2):
        pltpu.make_async_copy(ys_ref.at[pl.ds(0, MOVE_ROWS)], buf.at[slot, s], sem.at[slot]).wait()
    w = wts_ref[...]
    out_ref[...] = x1_ref[...] + w[:, 0:1] * buf[slot, 0] + w[:, 1:2] * buf[slot, 1]


def _combine(pos3, x1, wts, ys):
    t = x1.shape[0]
    steps = t // MOVE_ROWS
    return pl.pallas_call(
        _combine_kernel,
        out_shape=jax.ShapeDtypeStruct((t, D_MODEL), F32),
        grid=(steps,),
        in_specs=[
            pl.BlockSpec((1, 1, 2 * MOVE_ROWS), lambda i: (i, 0, 0), memory_space=pltpu.SMEM),
            pl.BlockSpec((1, 1, 2 * MOVE_ROWS), lambda i: (jnp.minimum(i + 1, steps - 1), 0, 0),
                         memory_space=pltpu.SMEM),
            pl.BlockSpec((MOVE_ROWS, D_MODEL), lambda i: (i, 0)),
            pl.BlockSpec((MOVE_ROWS, LANES), lambda i: (i, 0)),
            pl.BlockSpec(memory_space=pl.ANY),
        ],
        out_specs=pl.BlockSpec((MOVE_ROWS, D_MODEL), lambda i: (i, 0)),
        scratch_shapes=[pltpu.VMEM((2, 2, MOVE_ROWS, D_MODEL), F32),
                        pltpu.SemaphoreType.DMA((2,))],
        compiler_params=pltpu.CompilerParams(
            dimension_semantics=("arbitrary",), vmem_limit_bytes=VMEM_LIMIT),
        name="combine",
    )(pos3, pos3, x1, wts, ys)


def _routing_positions(ids, n_tiles):
    e = ids.reshape(-1)
    onehot = (e[:, None] == jnp.arange(N_EXPERTS, dtype=jnp.int32)[None, :]).astype(jnp.int32)
    csum = jnp.cumsum(onehot, axis=0)
    rank = jnp.sum(csum * onehot, axis=1) - 1
    counts = csum[-1]
    padded = ((counts + EXPERT_TILE - 1) // EXPERT_TILE) * EXPERT_TILE
    ends = jnp.cumsum(padded)
    starts = ends - padded
    pos = jnp.sum(starts[None, :] * onehot, axis=1) + rank
    tile_start = jnp.arange(n_tiles, dtype=jnp.int32) * EXPERT_TILE
    tile_expert = jnp.sum((tile_start[:, None] >= ends[None, :]).astype(jnp.int32), axis=1)
    tile_expert = jnp.minimum(tile_expert, N_EXPERTS - 1).astype(jnp.int32)
    n_valid = (ends[-1] // EXPERT_TILE).astype(jnp.int32).reshape(1)
    return pos.astype(jnp.int32), tile_expert, n_valid


def _rope_tables(seq):
    inv_freq = 1.0 / (ROPE_THETA ** (jnp.arange(0, 2 * ROPE_HALF, 2, dtype=F32) / (2 * ROPE_HALF)))
    ang = jnp.arange(seq, dtype=F32)[:, None] * inv_freq[None, :]
    cos, sin = jnp.cos(ang), jnp.sin(ang)
    ones = jnp.ones((seq, HEAD_DIM - 2 * ROPE_HALF), F32)
    zeros8 = jnp.zeros((seq, ROPE_HALF), F32)
    zeros = jnp.zeros((seq, HEAD_DIM - 2 * ROPE_HALF), F32)
    cos_h = jnp.concatenate([cos, cos, ones], axis=1)
    sa_h = jnp.concatenate([-sin, zeros8, zeros], axis=1)
    sb_h = jnp.concatenate([zeros8, sin, zeros], axis=1)
    rep = LANES // HEAD_DIM
    return jnp.tile(cos_h, (1, rep)), jnp.tile(sa_h, (1, rep)), jnp.tile(sb_h, (1, rep))


def kernel(x, norm1_g, w_in, q_norm_g, k_norm_g, lambda_q1, lambda_k1, lambda_q2, lambda_k2,
           da_out_g, sb_out_g, w_out, norm2_g, w_router_group, b_router_group,
           w_router_expert, b_router_expert, w_gate, w_up, w_down):
    batch, seq, d = x.shape
    depth = w_in.shape[0]
    t = batch * seq
    assert d == D_MODEL and seq % ROW_TILE == 0 and seq % max(DA_Q_ROWS, SB_Q_ROWS) == 0
    assert t % MOVE_ROWS == 0 and t % DISPATCH_ROWS == 0

    cos_t, sa_t, sb_t = _rope_tables(seq)
    head_of = jnp.arange(GROUP_WIDTH) // HEAD_DIM
    seg = (head_of[:, None] == head_of[None, :]).astype(BF16)
    kk = jnp.arange(ATT_BLOCK)
    tri = (kk[:, None] >= kk[None, :]).astype(BF16)
    tri = jnp.concatenate([tri, tri], axis=0)
    n_rows = 2 * t + N_EXPERTS * EXPERT_TILE
    n_tiles = n_rows // EXPERT_TILE
    heads_per_group = GROUP_WIDTH // HEAD_DIM

    x2 = x.reshape(t, d)
    xs = jnp.zeros((n_rows, d), F32)
    for l in range(depth):
        lambda_init = 0.8 - 0.6 * math.exp(-0.3 * l)
        lam = (jnp.exp(jnp.sum(lambda_q1[l] * lambda_k1[l]))
               - jnp.exp(jnp.sum(lambda_q2[l] * lambda_k2[l])) + lambda_init).reshape(1).astype(F32)

        daq, dak, dav, sbq, sbk, sbv = _inproj(
            x2, norm1_g[l][None, :], w_in[l].astype(BF16),
            jnp.tile(q_norm_g[l], heads_per_group)[None, :],
            jnp.tile(k_norm_g[l], heads_per_group)[None, :],
            cos_t, sa_t, sb_t, seg, seq)

        o_da = _diff_attention(lam, daq, dak, dav, da_out_g[l][None, :], batch, seq,
                               1.0 - lambda_init)
        o_sb = _stick_breaking_attention(sbq, sbk, sbv, tri, sb_out_g[l][None, :], batch, seq)

        wr = jnp.concatenate([w_router_group[l], w_router_expert[l]], axis=1)
        wr = jnp.pad(wr, ((0, 0), (0, LANES - wr.shape[1])))
        wr_hi, wr_lo = _split_bf16(wr)
        br = jnp.concatenate([b_router_group[l], b_router_expert[l]])
        br = jnp.pad(br, (0, LANES - br.shape[0]))[None, :]
        x1, h2, ids, wts = _outproj(x2, o_da, o_sb, w_out[l].astype(BF16), norm2_g[l][None, :],
                                    wr_hi, wr_lo, br)

        pos, tile_expert, n_valid = _routing_positions(ids[:, :2], n_tiles)
        pos3 = pos.reshape(t // MOVE_ROWS, 1, 2 * MOVE_ROWS)
        xs = _dispatch(pos, h2, xs)
        ys = _experts(tile_expert, n_valid, xs, w_gate, w_up, w_down, l)
        x2 = _combine(pos3, x1, wts, ys)
    return x2.reshape(batch, seq, d)
```

```python
import functools
import math

import jax
import jax.numpy as jnp
from jax import lax
from jax.experimental import pallas as pl
from jax.experimental.pallas import tpu as pltpu

F32 = jnp.float32
BF16 = jnp.bfloat16

D_MODEL = 1024
HEAD_DIM = 64
GROUP_WIDTH = 512
ROPE_HALF = 8
N_GROUPS = 4
EXPERTS_PER_GROUP = 8
N_EXPERTS = N_GROUPS * EXPERTS_PER_GROUP
EXPERT_FF = 256
RMS_EPS = 1e-6
ROPE_THETA = 500000.0
LANES = 128

ROW_TILE = 512
ATT_BLOCK = 256
SB_Q_ROWS = 1024
DA_Q_ROWS = 1024
DA_KEY_SPAN = 1024
EXPERT_TILE = 512
MOVE_ROWS = 512
DISPATCH_ROWS = 1024
VMEM_LIMIT = 48 * 1024 * 1024


def _split_bf16(x):
    hi = x.astype(BF16)
    lo = (x - hi.astype(F32)).astype(BF16)
    return hi, lo


def _inproj_kernel(x_ref, g1_ref, w_ref, qg_ref, kg_ref, cos_ref, sa_ref, sb_ref, seg_ref,
                   daq_ref, dak_ref, dav_ref, sbq_ref, sbk_ref, sbv_ref):
    x = x_ref[...]
    ms = jnp.mean(x * x, axis=-1, keepdims=True)
    h = (x * lax.rsqrt(ms + RMS_EPS) * g1_ref[...]).astype(BF16)

    def proj(c):
        return jnp.dot(h, w_ref[:, c * GROUP_WIDTH:(c + 1) * GROUP_WIDTH],
                       preferred_element_type=F32)

    cos_t = cos_ref[...]
    sa_t = sa_ref[...]
    sb_t = sb_ref[...]
    seg = seg_ref[...]

    def head_norm_rope(p, g):
        hi, lo = _split_bf16(p * p)
        ssum = (jnp.dot(hi, seg, preferred_element_type=F32)
                + jnp.dot(lo, seg, preferred_element_type=F32))
        n = p * lax.rsqrt(ssum * (1.0 / HEAD_DIM) + RMS_EPS) * g
        outs = []
        for c in range(GROUP_WIDTH // LANES):
            t = n[:, c * LANES:(c + 1) * LANES]
            up = pltpu.roll(t, LANES - ROPE_HALF, 1)
            dn = pltpu.roll(t, ROPE_HALF, 1)
            outs.append(t * cos_t + up * sa_t + dn * sb_t)
        return jnp.concatenate(outs, axis=1)

    scale = HEAD_DIM ** -0.5
    daq_ref[...] = (head_norm_rope(proj(0), qg_ref[...]) * scale).astype(BF16)
    dak_ref[...] = head_norm_rope(proj(1), kg_ref[...]).astype(BF16)
    dav_ref[...] = proj(2).astype(BF16)
    sbq_ref[...] = (proj(3) * scale).astype(BF16)
    sbk_ref[...] = proj(4).astype(BF16)
    sbv_ref[...] = proj(5).astype(BF16)


def _inproj(x2, g1, w_bf, qg, kg, cos_t, sa_t, sb_t, seg, seq):
    t = x2.shape[0]
    n_tiles = t // ROW_TILE
    seq_tiles = seq // ROW_TILE
    row = lambda i: (i, 0)
    const = lambda i: (0, 0)
    pos = lambda i: (i % seq_tiles, 0)
    out = jax.ShapeDtypeStruct((t, GROUP_WIDTH), BF16)
    return pl.pallas_call(
        _inproj_kernel,
        out_shape=[out] * 6,
        grid=(n_tiles,),
        in_specs=[
            pl.BlockSpec((ROW_TILE, D_MODEL), row),
            pl.BlockSpec((1, D_MODEL), const),
            pl.BlockSpec((D_MODEL, 6 * GROUP_WIDTH), const),
            pl.BlockSpec((1, GROUP_WIDTH), const),
            pl.BlockSpec((1, GROUP_WIDTH), const),
            pl.BlockSpec((ROW_TILE, LANES), pos),
            pl.BlockSpec((ROW_TILE, LANES), pos),
            pl.BlockSpec((ROW_TILE, LANES), pos),
            pl.BlockSpec((GROUP_WIDTH, GROUP_WIDTH), const),
        ],
        out_specs=[pl.BlockSpec((ROW_TILE, GROUP_WIDTH), row)] * 6,
        compiler_params=pltpu.CompilerParams(
            dimension_semantics=("parallel",), vmem_limit_bytes=VMEM_LIMIT),
        name="inproj",
    )(x2, g1, w_bf, qg, kg, cos_t, sa_t, sb_t, seg)


def _da_kernel(lam_ref, q_ref, k_ref, v_ref, g_ref, o_ref, m_ref, l_ref, acc_ref, *, out_scale):
    i = pl.program_id(2)
    blk = ATT_BLOCK
    nsub = DA_Q_ROWS // blk
    m_ref[...] = jnp.full_like(m_ref, -jnp.inf)
    l_ref[...] = jnp.zeros_like(l_ref)
    acc_ref[...] = jnp.zeros_like(acc_ref)

    def step(key0, span, first_chunk, masked):
        units = [(s, c) for s in range(2) for c in range(first_chunk, nsub)]
        if masked:
            row = lax.broadcasted_iota(jnp.int32, (blk, span), 0)
            col = lax.broadcasted_iota(jnp.int32, (blk, span), 1)
        kj = [k_ref[pl.ds(key0, span), s * HEAD_DIM:(s + 1) * HEAD_DIM] for s in range(2)]
        vj = v_ref[pl.ds(key0, span), :]
        sc = {}
        for s, c in units:
            qs = q_ref[c * blk:(c + 1) * blk, s * HEAD_DIM:(s + 1) * HEAD_DIM]
            sc[s, c] = lax.dot_general(qs, kj[s], (((1,), (1,)), ((), ())),
                                       preferred_element_type=F32)
            if masked:
                sc[s, c] = jnp.where(col <= row + (c - first_chunk) * blk, sc[s, c], -jnp.inf)
        wide = (blk, LANES)
        m_new = {}
        p = {}
        for u in units:
            parts = [sc[u][:, x:x + LANES] for x in range(0, span, LANES)]
            row_max = jnp.max(functools.reduce(jnp.maximum, parts), axis=-1, keepdims=True)
            m_new[u] = jnp.maximum(m_ref[u], jnp.broadcast_to(row_max, wide))
            p[u] = [jnp.exp(x - m_new[u]) for x in parts]
        pv = {u: jnp.dot(jnp.concatenate(p[u], axis=1).astype(BF16), vj,
                         preferred_element_type=F32) for u in units}
        for u in units:
            alpha = jnp.exp(m_ref[u] - m_new[u])
            m_ref[u] = m_new[u]
            l_ref[u] = alpha * l_ref[u] + functools.reduce(jnp.add, p[u])
            acc_ref[u] = alpha * acc_ref[u] + pv[u]

    def body(j, c):
        step(pl.multiple_of(j * DA_KEY_SPAN, DA_KEY_SPAN), DA_KEY_SPAN, 0, False)
        return c

    lax.fori_loop(0, i * (DA_Q_ROWS // DA_KEY_SPAN), body, 0)
    for d in range(nsub):
        step((i * nsub + d) * blk, blk, d, True)
    lam = lam_ref[0]
    outs = []
    for c in range(nsub):
        l0 = jnp.sum(l_ref[0, c], axis=-1, keepdims=True)
        l1 = jnp.sum(l_ref[1, c], axis=-1, keepdims=True)
        o = acc_ref[0, c] / l0 - lam * (acc_ref[1, c] / l1)
        ms = jnp.mean(o * o, axis=-1, keepdims=True)
        outs.append(o * lax.rsqrt(ms + RMS_EPS) * g_ref[...] * out_scale)
    o_ref[...] = jnp.concatenate(outs, axis=0).astype(BF16)


def _diff_attention(lam, q, k, v, g, batch, seq, out_scale):
    nq = seq // DA_Q_ROWS
    heads = GROUP_WIDTH // (2 * HEAD_DIM)
    return pl.pallas_call(
        functools.partial(_da_kernel, out_scale=out_scale),
        out_shape=jax.ShapeDtypeStruct(q.shape, BF16),
        grid=(batch, heads, nq),
        in_specs=[
            pl.BlockSpec(memory_space=pltpu.SMEM),
            pl.BlockSpec((DA_Q_ROWS, 2 * HEAD_DIM), lambda b, h, i: (b * nq + i, h)),
            pl.BlockSpec((seq, 2 * HEAD_DIM), lambda b, h, i: (b, h)),
            pl.BlockSpec((seq, 2 * HEAD_DIM), lambda b, h, i: (b, h)),
            pl.BlockSpec((1, 2 * HEAD_DIM), lambda b, h, i: (0, 0)),
        ],
        out_specs=pl.BlockSpec((DA_Q_ROWS, 2 * HEAD_DIM), lambda b, h, i: (b * nq + i, h)),
        scratch_shapes=[
            pltpu.VMEM((2, DA_Q_ROWS // ATT_BLOCK, ATT_BLOCK, LANES), F32),
            pltpu.VMEM((2, DA_Q_ROWS // ATT_BLOCK, ATT_BLOCK, LANES), F32),
            pltpu.VMEM((2, DA_Q_ROWS // ATT_BLOCK, ATT_BLOCK, 2 * HEAD_DIM), F32),
        ],
        compiler_params=pltpu.CompilerParams(
            dimension_semantics=("parallel", "parallel", "parallel"),
            vmem_limit_bytes=VMEM_LIMIT),
        name="diff_attention",
    )(lam, q, k, v, g)


def _sb_kernel(q_ref, k_ref, v_ref, tri2_ref, g_ref, o_ref, tail_ref, acc_ref):
    i = pl.program_id(2)
    blk = ATT_BLOCK
    nsub = SB_Q_ROWS // blk
    tri2 = tri2_ref[...]
    tail_ref[...] = jnp.zeros_like(tail_ref)
    acc_ref[...] = jnp.zeros_like(acc_ref)

    def step(j, first_chunk, masked):
        units = [(s, c) for s in range(2) for c in range(first_chunk, nsub)]
        if masked:
            row = lax.broadcasted_iota(jnp.int32, (blk, blk), 0)
            col = lax.broadcasted_iota(jnp.int32, (blk, blk), 1)
            offset = lambda c: col < row + (c - first_chunk) * blk
        kj = [k_ref[pl.ds(j * blk, blk), s * HEAD_DIM:(s + 1) * HEAD_DIM] for s in range(2)]
        vj = [v_ref[pl.ds(j * blk, blk), s * HEAD_DIM:(s + 1) * HEAD_DIM] for s in range(2)]
        z = {}
        for s, c in units:
            qs = q_ref[c * blk:(c + 1) * blk, s * HEAD_DIM:(s + 1) * HEAD_DIM]
            z[s, c] = lax.dot_general(qs, kj[s], (((1,), (1,)), ((), ())),
                                      preferred_element_type=F32)
        split = {}
        for u in units:
            sp = jnp.maximum(z[u], 0.0) + jnp.log(1.0 + jnp.exp(-jnp.abs(z[u])))
            if masked:
                sp = jnp.where(offset(u[1]), sp, 0.0)
            split[u] = jnp.concatenate(_split_bf16(sp), axis=1)
        suffix = {u: jnp.dot(split[u], tri2, preferred_element_type=F32) for u in units}
        a = {}
        for s, c in units:
            w = jnp.exp(z[s, c] - suffix[s, c] - tail_ref[s, c])
            if masked:
                w = jnp.where(offset(c), w, 0.0)
            a[s, c] = w.astype(BF16)
        for s, c in units:
            tail_ref[s, c] += suffix[s, c][:, 0:1]
            acc_ref[s, c] += jnp.dot(a[s, c], vj[s], preferred_element_type=F32)

    for d in reversed(range(nsub)):
        step(i * nsub + d, d, True)

    def body(t, c):
        for d in range(nsub):
            step((i - t) * nsub - 1 - d, 0, False)
        return c

    lax.fori_loop(0, i, body, 0)
    g = g_ref[...]
    outs = []
    for s in range(2):
        o = jnp.concatenate([acc_ref[s, c] for c in range(nsub)], axis=0)
        ms = jnp.mean(o * o, axis=-1, keepdims=True)
        outs.append(o * lax.rsqrt(ms + RMS_EPS) * g)
    o_ref[...] = jnp.concatenate(outs, axis=1).astype(BF16)


def _stick_breaking_attention(q, k, v, tri, g, batch, seq):
    nq = seq // SB_Q_ROWS
    pairs = GROUP_WIDTH // (2 * HEAD_DIM)
    return pl.pallas_call(
        _sb_kernel,
        out_shape=jax.ShapeDtypeStruct(q.shape, BF16),
        grid=(batch, pairs, nq),
        in_specs=[
            pl.BlockSpec((SB_Q_ROWS, 2 * HEAD_DIM), lambda b, h, i: (b * nq + i, h)),
            pl.BlockSpec((seq, 2 * HEAD_DIM), lambda b, h, i: (b, h)),
            pl.BlockSpec((seq, 2 * HEAD_DIM), lambda b, h, i: (b, h)),
            pl.BlockSpec((2 * ATT_BLOCK, ATT_BLOCK), lambda b, h, i: (0, 0)),
            pl.BlockSpec((1, HEAD_DIM), lambda b, h, i: (0, 0)),
        ],
        out_specs=pl.BlockSpec((SB_Q_ROWS, 2 * HEAD_DIM), lambda b, h, i: (b * nq + i, h)),
        scratch_shapes=[
            pltpu.VMEM((2, SB_Q_ROWS // ATT_BLOCK, ATT_BLOCK, 1), F32),
            pltpu.VMEM((2, SB_Q_ROWS // ATT_BLOCK, ATT_BLOCK, HEAD_DIM), F32),
        ],
        compiler_params=pltpu.CompilerParams(
            dimension_semantics=("parallel", "parallel", "parallel"),
            vmem_limit_bytes=VMEM_LIMIT),
        name="stick_breaking_attention",
    )(q, k, v, tri, g)


def _outproj_kernel(x_ref, oda_ref, osb_ref, w_ref, g2_ref, wrh_ref, wrl_ref, br_ref,
                    x1_ref, h2_ref, ids_ref, wts_ref):
    mix = (jnp.dot(oda_ref[...], w_ref[:GROUP_WIDTH, :], preferred_element_type=F32)
           + jnp.dot(osb_ref[...], w_ref[GROUP_WIDTH:, :], preferred_element_type=F32))
    x1 = x_ref[...] + mix
    x1_ref[...] = x1
    ms = jnp.mean(x1 * x1, axis=-1, keepdims=True)
    h2 = x1 * lax.rsqrt(ms + RMS_EPS) * g2_ref[...]
    h2_ref[...] = h2

    hi, lo = _split_bf16(h2)
    wrh = wrh_ref[...]
    logits = (jnp.dot(hi, wrh, preferred_element_type=F32)
              + jnp.dot(hi, wrl_ref[...], preferred_element_type=F32)
              + jnp.dot(lo, wrh, preferred_element_type=F32)) + br_ref[...]

    rows = logits.shape[0]
    lane = lax.broadcasted_iota(jnp.int32, (rows, LANES), 1).astype(F32)
    neg = -jnp.inf
    gl = jnp.where(lane < N_GROUPS, logits, neg)
    gmax = jnp.max(gl, axis=-1, keepdims=True)
    g_p = 1.0 / jnp.sum(jnp.exp(gl - gmax), axis=-1, keepdims=True)
    g_idx = jnp.min(jnp.where(gl == gmax, lane, float(LANES)), axis=-1, keepdims=True)
    first = N_GROUPS + g_idx * EXPERTS_PER_GROUP
    el = jnp.where((lane >= first) & (lane < first + EXPERTS_PER_GROUP), logits, neg)
    e1 = jnp.max(el, axis=-1, keepdims=True)
    i1 = jnp.min(jnp.where(el == e1, lane, float(LANES)), axis=-1, keepdims=True)
    el2 = jnp.where(lane == i1, neg, el)
    e2 = jnp.max(el2, axis=-1, keepdims=True)
    i2 = jnp.min(jnp.where(el2 == e2, lane, float(LANES)), axis=-1, keepdims=True)
    d = jnp.exp(e2 - e1)
    w1 = g_p / (1.0 + d)
    w2 = g_p * d / (1.0 + d)
    ids = jnp.where(lane == 0.0, i1 - N_GROUPS, jnp.where(lane == 1.0, i2 - N_GROUPS, 0.0))
    ids_ref[...] = ids.astype(jnp.int32)
    wts_ref[...] = jnp.where(lane == 0.0, w1, jnp.where(lane == 1.0, w2, 0.0))


def _outproj(x2, oda, osb, w_bf, g2, wr_hi, wr_lo, br):
    t = x2.shape[0]
    row = lambda i: (i, 0)
    const = lambda i: (0, 0)
    return pl.pallas_call(
        _outproj_kernel,
        out_shape=[
            jax.ShapeDtypeStruct((t, D_MODEL), F32),
            jax.ShapeDtypeStruct((t, D_MODEL), F32),
            jax.ShapeDtypeStruct((t, LANES), jnp.int32),
            jax.ShapeDtypeStruct((t, LANES), F32),
        ],
        grid=(t // ROW_TILE,),
        in_specs=[
            pl.BlockSpec((ROW_TILE, D_MODEL), row),
            pl.BlockSpec((ROW_TILE, GROUP_WIDTH), row),
            pl.BlockSpec((ROW_TILE, GROUP_WIDTH), row),
            pl.BlockSpec((D_MODEL, D_MODEL), const),
            pl.BlockSpec((1, D_MODEL), const),
            pl.BlockSpec((D_MODEL, LANES), const),
            pl.BlockSpec((D_MODEL, LANES), const),
            pl.BlockSpec((1, LANES), const),
        ],
        out_specs=[
            pl.BlockSpec((ROW_TILE, D_MODEL), row),
            pl.BlockSpec((ROW_TILE, D_MODEL), row),
            pl.BlockSpec((ROW_TILE, LANES), row),
            pl.BlockSpec((ROW_TILE, LANES), row),
        ],
        compiler_params=pltpu.CompilerParams(
            dimension_semantics=("parallel",), vmem_limit_bytes=VMEM_LIMIT),
        name="outproj_router",
    )(x2, oda, osb, w_bf, g2, wr_hi, wr_lo, br)


def _row_copy(src, src_row, dst, dst_row, sem):
    return pltpu.make_async_copy(src.at[pl.ds(src_row, 1)], dst.at[pl.ds(dst_row, 1)], sem)


def _dispatch_kernel(pos_ref, h_ref, zero_ref, xs_ref, sem):
    del zero_ref

    def issue(r, c):
        for s in range(2):
            _row_copy(h_ref, r, xs_ref, pos_ref[0, 0, 2 * r + s], sem).start(priority=s)
        return c

    lax.fori_loop(0, DISPATCH_ROWS, issue, 0, unroll=4)
    for s in range(2):
        pltpu.make_async_copy(h_ref, xs_ref.at[pl.ds(0, DISPATCH_ROWS)], sem).wait()


def _dispatch(pos, h2, zeros):
    t = h2.shape[0]
    return pl.pallas_call(
        _dispatch_kernel,
        out_shape=jax.ShapeDtypeStruct(zeros.shape, F32),
        grid=(t // DISPATCH_ROWS,),
        in_specs=[
            pl.BlockSpec((1, 1, 2 * DISPATCH_ROWS), lambda i: (i, 0, 0), memory_space=pltpu.SMEM),
            pl.BlockSpec((DISPATCH_ROWS, D_MODEL), lambda i: (i, 0)),
            pl.BlockSpec(memory_space=pl.ANY),
        ],
        out_specs=pl.BlockSpec(memory_space=pl.ANY),
        scratch_shapes=[pltpu.SemaphoreType.DMA(())],
        input_output_aliases={2: 0},
        compiler_params=pltpu.CompilerParams(
            dimension_semantics=("arbitrary",), has_side_effects=True,
            vmem_limit_bytes=VMEM_LIMIT),
        name="dispatch",
    )(pos.reshape(t // DISPATCH_ROWS, 1, 2 * DISPATCH_ROWS), h2, zeros)


def _experts_kernel(te_ref, nv_ref, xs_ref, wg_ref, wu_ref, wd_ref, ys_ref):
    i = pl.program_id(0)

    @pl.when(i < nv_ref[0])
    def _():
        x = xs_ref[...].astype(BF16)
        hg = jnp.dot(x, wg_ref[0, 0].astype(BF16), preferred_element_type=F32)
        hu = jnp.dot(x, wu_ref[0, 0].astype(BF16), preferred_element_type=F32)
        act = (hg * jax.nn.sigmoid(hg) * hu).astype(BF16)
        ys_ref[...] = jnp.dot(act, wd_ref[0, 0].astype(BF16), preferred_element_type=F32)

    @pl.when(i >= nv_ref[0])
    def _():
        ys_ref[...] = jnp.zeros_like(ys_ref)


def _experts(tile_expert, n_valid, xs, w_gate, w_up, w_down, layer):
    p = xs.shape[0]
    grid_spec = pltpu.PrefetchScalarGridSpec(
        num_scalar_prefetch=2,
        grid=(p // EXPERT_TILE,),
        in_specs=[
            pl.BlockSpec((EXPERT_TILE, D_MODEL), lambda i, te, nv: (i, 0)),
            pl.BlockSpec((1, 1, D_MODEL, EXPERT_FF), lambda i, te, nv: (layer, te[i], 0, 0)),
            pl.BlockSpec((1, 1, D_MODEL, EXPERT_FF), lambda i, te, nv: (layer, te[i], 0, 0)),
            pl.BlockSpec((1, 1, EXPERT_FF, D_MODEL), lambda i, te, nv: (layer, te[i], 0, 0)),
        ],
        out_specs=pl.BlockSpec((EXPERT_TILE, D_MODEL), lambda i, te, nv: (i, 0)),
    )
    return pl.pallas_call(
        _experts_kernel,
        out_shape=jax.ShapeDtypeStruct((p, D_MODEL), F32),
        grid_spec=grid_spec,
        compiler_params=pltpu.CompilerParams(
            dimension_semantics=("arbitrary",), vmem_limit_bytes=VMEM_LIMIT),
        name="experts",
    )(tile_expert, n_valid, xs, w_gate, w_up, w_down)


def _combine_kernel(pos_ref, pos_next_ref, x1_ref, wts_ref, ys_ref, out_ref, buf, sem):
    i = pl.program_id(0)
    n = pl.num_programs(0)

    def gather(p_ref, slot):
        def issue(r, c):
            for s in range(2):
                _row_copy(ys_ref, p_ref[0, 0, 2 * r + s], buf.at[slot, s], r,
                          sem.at[slot]).start(priority=s)
            return c
        lax.fori_loop(0, MOVE_ROWS, issue, 0, unroll=4)

    @pl.when(i == 0)
    def _():
        gather(pos_ref, 0)

    @pl.when(i + 1 < n)
    def _():
        gather(pos_next_ref, (i + 1) % 2)

    slot = i % 2
    for s in range(2):
        pltpu.make_async_copy(ys_ref.at[pl.ds(0, MOVE_ROWS)], buf.at[slot, s], sem.at[slot]).wait()
    w = wts_ref[...]
    out_ref[...] = x1_ref[...] + w[:, 0:1] * buf[slot, 0] + w[:, 1:2] * buf[slot, 1]


def _combine(pos3, x1, wts, ys):
    t = x1.shape[0]
    steps = t // MOVE_ROWS
    return pl.pallas_call(
        _combine_kernel,
        out_shape=jax.ShapeDtypeStruct((t, D_MODEL), F32),
        grid=(steps,),
        in_specs=[
            pl.BlockSpec((1, 1, 2 * MOVE_ROWS), lambda i: (i, 0, 0), memory_space=pltpu.SMEM),
            pl.BlockSpec((1, 1, 2 * MOVE_ROWS), lambda i: (jnp.minimum(i + 1, steps - 1), 0, 0),
                         memory_space=pltpu.SMEM),
            pl.BlockSpec((MOVE_ROWS, D_MODEL), lambda i: (i, 0)),
            pl.BlockSpec((MOVE_ROWS, LANES), lambda i: (i, 0)),
            pl.BlockSpec(memory_space=pl.ANY),
        ],
        out_specs=pl.BlockSpec((MOVE_ROWS, D_MODEL), lambda i: (i, 0)),
        scratch_shapes=[pltpu.VMEM((2, 2, MOVE_ROWS, D_MODEL), F32),
                        pltpu.SemaphoreType.DMA((2,))],
        compiler_params=pltpu.CompilerParams(
            dimension_semantics=("arbitrary",), vmem_limit_bytes=VMEM_LIMIT),
        name="combine",
    )(pos3, pos3, x1, wts, ys)


def _routing_positions(ids, n_tiles):
    e = ids.reshape(-1)
    onehot = (e[:, None] == jnp.arange(N_EXPERTS, dtype=jnp.int32)[None, :]).astype(jnp.int32)
    csum = jnp.cumsum(onehot, axis=0)
    rank = jnp.sum(csum * onehot, axis=1) - 1
    counts = csum[-1]
    padded = ((counts + EXPERT_TILE - 1) // EXPERT_TILE) * EXPERT_TILE
    ends = jnp.cumsum(padded)
    starts = ends - padded
    pos = jnp.sum(starts[None, :] * onehot, axis=1) + rank
    tile_start = jnp.arange(n_tiles, dtype=jnp.int32) * EXPERT_TILE
    tile_expert = jnp.sum((tile_start[:, None] >= ends[None, :]).astype(jnp.int32), axis=1)
    tile_expert = jnp.minimum(tile_expert, N_EXPERTS - 1).astype(jnp.int32)
    n_valid = (ends[-1] // EXPERT_TILE).astype(jnp.int32).reshape(1)
    return pos.astype(jnp.int32), tile_expert, n_valid


def _rope_tables(seq):
    inv_freq = 1.0 / (ROPE_THETA ** (jnp.arange(0, 2 * ROPE_HALF, 2, dtype=F32) / (2 * ROPE_HALF)))
    ang = jnp.arange(seq, dtype=F32)[:, None] * inv_freq[None, :]
    cos, sin = jnp.cos(ang), jnp.sin(ang)
    ones = jnp.ones((seq, HEAD_DIM - 2 * ROPE_HALF), F32)
    zeros8 = jnp.zeros((seq, ROPE_HALF), F32)
    zeros = jnp.zeros((seq, HEAD_DIM - 2 * ROPE_HALF), F32)
    cos_h = jnp.concatenate([cos, cos, ones], axis=1)
    sa_h = jnp.concatenate([-sin, zeros8, zeros], axis=1)
    sb_h = jnp.concatenate([zeros8, sin, zeros], axis=1)
    rep = LANES // HEAD_DIM
    return jnp.tile(cos_h, (1, rep)), jnp.tile(sa_h, (1, rep)), jnp.tile(sb_h, (1, rep))


def kernel(x, norm1_g, w_in, q_norm_g, k_norm_g, lambda_q1, lambda_k1, lambda_q2, lambda_k2,
           da_out_g, sb_out_g, w_out, norm2_g, w_router_group, b_router_group,
           w_router_expert, b_router_expert, w_gate, w_up, w_down):
    batch, seq, d = x.shape
    depth = w_in.shape[0]
    t = batch * seq
    assert d == D_MODEL and seq % ROW_TILE == 0 and seq % max(DA_Q_ROWS, SB_Q_ROWS) == 0
    assert t % MOVE_ROWS == 0 and t % DISPATCH_ROWS == 0

    cos_t, sa_t, sb_t = _rope_tables(seq)
    head_of = jnp.arange(GROUP_WIDTH) // HEAD_DIM
    seg = (head_of[:, None] == head_of[None, :]).astype(BF16)
    kk = jnp.arange(ATT_BLOCK)
    tri = (kk[:, None] >= kk[None, :]).astype(BF16)
    tri = jnp.concatenate([tri, tri], axis=0)
    n_rows = 2 * t + N_EXPERTS * EXPERT_TILE
    n_tiles = n_rows // EXPERT_TILE
    heads_per_group = GROUP_WIDTH // HEAD_DIM

    x2 = x.reshape(t, d)
    xs = jnp.zeros((n_rows, d), F32)
    for l in range(depth):
        lambda_init = 0.8 - 0.6 * math.exp(-0.3 * l)
        lam = (jnp.exp(jnp.sum(lambda_q1[l] * lambda_k1[l]))
               - jnp.exp(jnp.sum(lambda_q2[l] * lambda_k2[l])) + lambda_init).reshape(1).astype(F32)

        daq, dak, dav, sbq, sbk, sbv = _inproj(
            x2, norm1_g[l][None, :], w_in[l].astype(BF16),
            jnp.tile(q_norm_g[l], heads_per_group)[None, :],
            jnp.tile(k_norm_g[l], heads_per_group)[None, :],
            cos_t, sa_t, sb_t, seg, seq)

        o_da = _diff_attention(lam, daq, dak, dav, da_out_g[l][None, :], batch, seq,
                               1.0 - lambda_init)
        o_sb = _stick_breaking_attention(sbq, sbk, sbv, tri, sb_out_g[l][None, :], batch, seq)

        wr = jnp.concatenate([w_router_group[l], w_router_expert[l]], axis=1)
        wr = jnp.pad(wr, ((0, 0), (0, LANES - wr.shape[1])))
        wr_hi, wr_lo = _split_bf16(wr)
        br = jnp.concatenate([b_router_group[l], b_router_expert[l]])
        br = jnp.pad(br, (0, LANES - br.shape[0]))[None, :]
        x1, h2, ids, wts = _outproj(x2, o_da, o_sb, w_out[l].astype(BF16), norm2_g[l][None, :],
                                    wr_hi, wr_lo, br)

        pos, tile_expert, n_valid = _routing_positions(ids[:, :2], n_tiles)
        pos3 = pos.reshape(t // MOVE_ROWS, 1, 2 * MOVE_ROWS)
        xs = _dispatch(pos, h2, xs)
        ys = _experts(tile_expert, n_valid, xs, w_gate, w_up, w_down, l)
        x2 = _combine(pos3, x1, wts, ys)
    return x2.reshape(batch, seq, d)
```
